```python
import math
import jax
import jax.numpy as jnp
from jax import lax
import numpy as np

D_MODEL = 1024
BATCH = 2
SEQ = 8192
DEPTH = 4
DEC_BATCH = 32
DEC_SEQ = 8
PAST_LEN = 8192
PAGE_SIZE = 128

HEAD_DIM = 64
H_FOX = 8
H_RWKV = 8
H_MOBA = 8
W_FOX = H_FOX * HEAD_DIM
W_RWKV = H_RWKV * HEAD_DIM
W_MOBA = H_MOBA * HEAD_DIM
RWKV_W_RANK = 64
RWKV_A_RANK = 64
RWKV_G_RANK = 128
RWKV_COLS = 3 * W_RWKV + RWKV_W_RANK + RWKV_A_RANK + RWKV_G_RANK
N_BRANCH = 3
IN_COLS = 3 * W_FOX + H_FOX + 3 * W_MOBA + RWKV_COLS + N_BRANCH * D_MODEL
D_FF = 4 * D_MODEL
MOBA_BLOCK = 256
MOBA_TOPK = 3
FOX_Q_BLOCK = 128
MOBA_Q_BLOCK = 32
ROPE_THETA = 500000.0
ROPE_DIM = HEAD_DIM // 4
NORM_EPS = 1e-6
LNX_EPS = 64e-5
L2_EPS = 1e-12

kernel_name = 'hybrid_fox_rwkv7_moba_step'


def split_cols(u, sizes):
    parts, start = [], 0
    for s in sizes:
        parts.append(u[..., start:start + s])
        start += s
    return parts


def rms_norm(x, g):
    xf = x.astype(jnp.float32)
    y = xf * lax.rsqrt(jnp.mean(xf * xf, axis=-1, keepdims=True) + NORM_EPS)
    return (y * g.astype(jnp.float32)).astype(x.dtype)


def partial_rotary(x, pos):
    half = ROPE_DIM // 2
    inv = ROPE_THETA ** (-jnp.arange(half, dtype=jnp.float32) / half)
    ang = pos.astype(jnp.float32)[:, None] * inv[None, :]
    cos = jnp.cos(ang)[None, :, None, :]
    sin = jnp.sin(ang)[None, :, None, :]
    xr = x[..., :ROPE_DIM].astype(jnp.float32)
    x1, x2 = xr[..., :half], xr[..., half:]
    rot = jnp.concatenate([x1 * cos - x2 * sin, x2 * cos + x1 * sin], axis=-1).astype(x.dtype)
    return jnp.concatenate([rot, x[..., ROPE_DIM:]], axis=-1)


def query_block(t, blk):
    return t if t <= blk else math.gcd(t, blk)


def fox_attention(q, k, v, c_q, c_k, q_pos):
    B, Tq, H, HD = q.shape
    L = k.shape[1]
    qb = query_block(Tq, FOX_Q_BLOCK)
    nb = Tq // qb
    k_pos = jnp.arange(L, dtype=jnp.int32)
    c_k_h = c_k.transpose(0, 2, 1)
    scale = HD ** -0.5

    def one_block(args):
        q_i, c_i, p_i = args
        s = jnp.einsum('bqhd,bkhd->bhqk', q_i, k, preferred_element_type=jnp.float32) * scale
        s = s + c_i.transpose(0, 2, 1)[..., None] - c_k_h[:, :, None, :]
        s = jnp.where(k_pos[None, None, None, :] <= p_i[None, None, :, None], s, -jnp.inf)
        p = jax.nn.softmax(s, axis=-1)
        return jnp.einsum('bhqk,bkhd->bqhd', p.astype(v.dtype), v)

    qs = q.reshape(B, nb, qb, H, HD).swapaxes(0, 1)
    cs = c_q.reshape(B, nb, qb, H).swapaxes(0, 1)
    ps = q_pos.reshape(nb, qb)
    out = lax.map(one_block, (qs, cs, ps))
    return out.swapaxes(0, 1).reshape(B, Tq, H, HD)


def moba_attention(q, k, v, q_pos):
    B, Tq, H, HD = q.shape
    L = k.shape[1]
    n_blk = -(-L // MOBA_BLOCK)
    pad = n_blk * MOBA_BLOCK - L
    kb = jnp.pad(k, ((0, 0), (0, pad), (0, 0), (0, 0))).reshape(B, n_blk, MOBA_BLOCK, H, HD).transpose(0, 3, 1, 2, 4)
    vb = jnp.pad(v, ((0, 0), (0, pad), (0, 0), (0, 0))).reshape(B, n_blk, MOBA_BLOCK, H, HD).transpose(0, 3, 1, 2, 4)
    k_mean = jnp.mean(kb.astype(jnp.float32), axis=3)
    top = min(MOBA_TOPK, n_blk)
    blk_id = jnp.arange(n_blk, dtype=jnp.int32)
    offs = jnp.arange(MOBA_BLOCK, dtype=jnp.int32)
    b_ix = jnp.arange(B)[:, None, None]
    h_ix = jnp.arange(H)[None, :, None]
    scale = HD ** -0.5
    qb = query_block(Tq, MOBA_Q_BLOCK)
    nb = Tq // qb

    def one_block(args):
        q_i, p_i = args
        qh = q_i.transpose(0, 2, 1, 3)
        own = p_i // MOBA_BLOCK
        gate = jnp.einsum('bhqd,bhnd->bhqn', qh.astype(jnp.float32), k_mean)
        gate = jnp.where(blk_id[None, None, None, :] < own[None, None, :, None], gate, -jnp.inf)
        g_val, sel = lax.top_k(gate, top)
        k_sel = kb[b_ix[..., None], h_ix[..., None], sel]
        v_sel = vb[b_ix[..., None], h_ix[..., None], sel]
        own_ix = jnp.broadcast_to(own[None, None, :], (B, H, qb))
        k_own = kb[b_ix, h_ix, own_ix]
        v_own = vb[b_ix, h_ix, own_ix]
        s_sel = jnp.einsum('bhqd,bhqnkd->bhqnk', qh, k_sel, preferred_element_type=jnp.float32) * scale
        s_sel = jnp.where(jnp.isfinite(g_val)[..., None], s_sel, -jnp.inf).reshape(B, H, qb, top * MOBA_BLOCK)
        s_own = jnp.einsum('bhqd,bhqkd->bhqk', qh, k_own, preferred_element_type=jnp.float32) * scale
        own_pos = own[:, None] * MOBA_BLOCK + offs[None, :]
        s_own = jnp.where(own_pos[None, None] <= p_i[None, None, :, None], s_own, -jnp.inf)
        p = jax.nn.softmax(jnp.concatenate([s_sel, s_own], axis=-1), axis=-1)
        p_sel = p[..., :top * MOBA_BLOCK].reshape(B, H, qb, top, MOBA_BLOCK).astype(v.dtype)
        p_own = p[..., top * MOBA_BLOCK:].astype(v.dtype)
        o = jnp.einsum('bhqnk,bhqnkd->bhqd', p_sel, v_sel) + jnp.einsum('bhqk,bhqkd->bhqd', p_own, v_own)
        return o.transpose(0, 2, 1, 3)

    qs = q.reshape(B, nb, qb, H, HD).swapaxes(0, 1)
    ps = q_pos.reshape(nb, qb)
    out = lax.map(one_block, (qs, ps))
    return out.swapaxes(0, 1).reshape(B, Tq, H, HD)


def rwkv7_time_mix(u, shift0, s0, mu, w0, w_up, a0, a_up, g_up, k_k, k_a, r_k, lnx_w, lnx_b):
    B, T, _ = u.shape
    H, N = H_RWKV, HEAD_DIM
    u_prev = jnp.concatenate([shift0[:, None, :].astype(u.dtype), u[:, :-1]], axis=1)
    um = u + (u_prev - u) * mu
    r, k, v, wd, ad, gd = split_cols(um, [W_RWKV, W_RWKV, W_RWKV, RWKV_W_RANK, RWKV_A_RANK, RWKV_G_RANK])
    w_log = -jax.nn.softplus(-(w0 + jnp.tanh(wd) @ w_up)) - 0.5
    decay = jnp.exp(-jnp.exp(w_log.astype(jnp.float32)))
    a = jax.nn.sigmoid(a0 + ad @ a_up)
    g = jax.nn.sigmoid(gd) @ g_up
    kk = (k * k_k).reshape(B, T, H, N).astype(jnp.float32)
    kk = kk * lax.rsqrt(jnp.sum(kk * kk, axis=-1, keepdims=True) + L2_EPS)
    k = k * (1.0 + (a - 1.0) * k_a)

    def heads(t):
        return t.reshape(B, T, H, N).astype(jnp.float32)

    r_h, k_h, v_h, a_h, w_h = heads(r), heads(k), heads(v), heads(a), heads(decay)

    def step(S, xs):
        r_t, w_t, k_t, v_t, kk_t, b_t = xs
        sa = jnp.einsum('bhvk,bhk->bhv', S, -kk_t)
        S = S * w_t[:, :, None, :] + sa[..., None] * b_t[:, :, None, :] + v_t[..., None] * k_t[:, :, None, :]
        return S, jnp.einsum('bhvk,bhk->bhv', S, r_t)

    seq = tuple(t.swapaxes(0, 1) for t in (r_h, w_h, k_h, v_h, kk, kk * a_h))
    s_fin, ys = lax.scan(step, s0.astype(jnp.float32), seq)
    y = ys.swapaxes(0, 1)
    mean = jnp.mean(y, axis=-1, keepdims=True)
    var = jnp.mean(jnp.square(y - mean), axis=-1, keepdims=True)
    y = ((y - mean) * lax.rsqrt(var + LNX_EPS)).reshape(B, T, W_RWKV) * lnx_w + lnx_b
    bonus = jnp.sum(r_h * k_h * r_k, axis=-1, keepdims=True) * v_h
    y = (y + bonus.reshape(B, T, W_RWKV)) * g
    return y.astype(u.dtype), u[:, -1], s_fin


def layer_forward(x, pos, past, lp):
    B, T, _ = x.shape
    h = rms_norm(x, lp['norm_mix'])
    u = h @ lp['w_in']
    fq, fk, fv, ff, mq, mk, mv, ur, ug = split_cols(
        u, [W_FOX, W_FOX, W_FOX, H_FOX, W_MOBA, W_MOBA, W_MOBA, RWKV_COLS, N_BRANCH * D_MODEL])
    fq = rms_norm(fq.reshape(B, T, H_FOX, HEAD_DIM), lp['qn_fox'])
    fk = rms_norm(fk.reshape(B, T, H_FOX, HEAD_DIM), lp['kn_fox'])
    fv = fv.reshape(B, T, H_FOX, HEAD_DIM)
    logf = jax.nn.log_sigmoid((ff + lp['b_forget']).astype(jnp.float32))
    mq = partial_rotary(rms_norm(mq.reshape(B, T, H_MOBA, HEAD_DIM), lp['qn_moba']), pos)
    mk = partial_rotary(rms_norm(mk.reshape(B, T, H_MOBA, HEAD_DIM), lp['kn_moba']), pos)
    mv = mv.reshape(B, T, H_MOBA, HEAD_DIM)
    if past is None:
        fk_all, fv_all, logf_all, mk_all, mv_all = fk, fv, logf, mk, mv
        shift0 = jnp.zeros((B, RWKV_COLS), x.dtype)
        s0 = jnp.zeros((B, H_RWKV, HEAD_DIM, HEAD_DIM), jnp.float32)
    else:
        fk_all = jnp.concatenate([past['fox_k'].astype(fk.dtype), fk], axis=1)
        fv_all = jnp.concatenate([past['fox_v'].astype(fv.dtype), fv], axis=1)
        logf_all = jnp.concatenate([past['fox_logf'].astype(jnp.float32), logf], axis=1)
        mk_all = jnp.concatenate([past['moba_k'].astype(mk.dtype), mk], axis=1)
        mv_all = jnp.concatenate([past['moba_v'].astype(mv.dtype), mv], axis=1)
        shift0, s0 = past['rwkv_shift'], past['rwkv']
    c_all = jnp.cumsum(logf_all, axis=1)
    y_fox = fox_attention(fq, fk_all, fv_all, c_all[:, -T:], c_all, pos)
    y_moba = moba_attention(mq, mk_all, mv_all, pos)
    y_rwkv, shift_new, s_new = rwkv7_time_mix(
        ur, shift0, s0, lp['rwkv_mu'], lp['rwkv_w0'], lp['rwkv_w_up'], lp['rwkv_a0'], lp['rwkv_a_up'],
        lp['rwkv_g_up'], lp['rwkv_k_k'], lp['rwkv_k_a'], lp['rwkv_r_k'], lp['rwkv_lnx_w'], lp['rwkv_lnx_b'])
    gates = jax.nn.sigmoid(ug.reshape(B, T, N_BRANCH, D_MODEL))
    merged = (gates[:, :, 0] * (y_fox.reshape(B, T, W_FOX) @ lp['w_br_fox'])
              + gates[:, :, 1] * (y_rwkv @ lp['w_br_rwkv'])
              + gates[:, :, 2] * (y_moba.reshape(B, T, W_MOBA) @ lp['w_br_moba']))
    x = x + merged @ lp['w_out']
    h2 = rms_norm(x, lp['norm_mlp'])
    x = x + jnp.square(jax.nn.relu(h2 @ lp['w_mlp_up'])) @ lp['w_mlp_down']
    return x, (fk, fv, logf, mk, mv, s_new, shift_new)


def setup_inputs(seed: int = 0) -> dict:
    key = jax.random.key(seed)
    keys = iter(jax.random.split(key, 64))

    def normal(shape, scale):
        return jax.random.normal(next(keys), shape, jnp.float32) * scale

    def uniform(shape, lo, hi):
        return jax.random.uniform(next(keys), shape, jnp.float32, lo, hi)

    def gain(shape):
        return 1.0 + normal(shape, 0.02)

    n_pages = PAST_LEN // PAGE_SIZE
    n_used = DEC_BATCH * n_pages
    n_pool = (5 * n_used + 3) // 4
    page_table = jax.random.permutation(next(keys), n_pool)[:n_used].reshape(DEC_BATCH, n_pages).astype(jnp.int32)
    kv_f = (n_pool, DEPTH, PAGE_SIZE, H_FOX, HEAD_DIM)
    kv_m = (n_pool, DEPTH, PAGE_SIZE, H_MOBA, HEAD_DIM)
    return {
        'x_prompt': normal((BATCH, SEQ, D_MODEL), 1.0),
        'x_sample': normal((DEC_BATCH, DEC_SEQ, D_MODEL), 1.0),
        'cache_fox_k': normal(kv_f, 1.0),
        'cache_fox_v': normal(kv_f, 1.0),
        'cache_fox_logf': jax.nn.log_sigmoid(normal((n_pool, DEPTH, PAGE_SIZE, H_FOX), 1.0) + 2.5),
        'cache_moba_k': normal(kv_m, 1.0),
        'cache_moba_v': normal(kv_m, 1.0),
        'state_rwkv': normal((DEC_BATCH, DEPTH, H_RWKV, HEAD_DIM, HEAD_DIM), 0.3),
        'state_rwkv_shift': normal((DEC_BATCH, DEPTH, RWKV_COLS), 1.0),
        'page_table': page_table,
        'norm_mix': gain((DEPTH, D_MODEL)),
        'w_in': normal((DEPTH, D_MODEL, IN_COLS), D_MODEL ** -0.5),
        'b_forget': uniform((DEPTH, H_FOX), 1.0, 4.0),
        'qn_fox': gain((DEPTH, HEAD_DIM)),
        'kn_fox': gain((DEPTH, HEAD_DIM)),
        'qn_moba': gain((DEPTH, HEAD_DIM)),
        'kn_moba': gain((DEPTH, HEAD_DIM)),
        'rwkv_mu': uniform((DEPTH, RWKV_COLS), 0.0, 1.0),
        'rwkv_w0': uniform((DEPTH, W_RWKV), -6.0, -1.0),
        'rwkv_w_up': normal((DEPTH, RWKV_W_RANK, W_RWKV), 0.1 * RWKV_W_RANK ** -0.5),
        'rwkv_a0': normal((DEPTH, W_RWKV), 0.1),
        'rwkv_a_up': normal((DEPTH, RWKV_A_RANK, W_RWKV), RWKV_A_RANK ** -0.5),
        'rwkv_g_up': normal((DEPTH, RWKV_G_RANK, W_RWKV), RWKV_G_RANK ** -0.5),
        'rwkv_k_k': 0.85 + normal((DEPTH, W_RWKV), 0.05),
        'rwkv_k_a': 1.0 + normal((DEPTH, W_RWKV), 0.05),
        'rwkv_r_k': normal((DEPTH, H_RWKV, HEAD_DIM), 0.1),
        'rwkv_lnx_w': gain((DEPTH, W_RWKV)),
        'rwkv_lnx_b': normal((DEPTH, W_RWKV), 0.02),
        'w_br_fox': normal((DEPTH, W_FOX, D_MODEL), W_FOX ** -0.5),
        'w_br_rwkv': normal((DEPTH, W_RWKV, D_MODEL), W_RWKV ** -0.5),
        'w_br_moba': normal((DEPTH, W_MOBA, D_MODEL), W_MOBA ** -0.5),
        'w_out': normal((DEPTH, D_MODEL, D_MODEL), D_MODEL ** -0.5),
        'norm_mlp': gain((DEPTH, D_MODEL)),
        'w_mlp_up': normal((DEPTH, D_MODEL, D_FF), D_MODEL ** -0.5),
        'w_mlp_down': normal((DEPTH, D_FF, D_MODEL), 0.5 * D_FF ** -0.5),
    }


def reference(x_prompt, x_sample, cache_fox_k, cache_fox_v, cache_fox_logf, cache_moba_k, cache_moba_v,
              state_rwkv, state_rwkv_shift, page_table,
              norm_mix, w_in, b_forget, qn_fox, kn_fox, qn_moba, kn_moba,
              rwkv_mu, rwkv_w0, rwkv_w_up, rwkv_a0, rwkv_a_up, rwkv_g_up, rwkv_k_k, rwkv_k_a, rwkv_r_k,
              rwkv_lnx_w, rwkv_lnx_b, w_br_fox, w_br_rwkv, w_br_moba, w_out, norm_mlp, w_mlp_up, w_mlp_down):
    db, ds = x_sample.shape[0], x_sample.shape[1]
    past_len = page_table.shape[1] * cache_fox_k.shape[2]
    pos_p = jnp.arange(x_prompt.shape[1], dtype=jnp.int32)
    pos_s = past_len + jnp.arange(ds, dtype=jnp.int32)
    xp, xs = x_prompt, x_sample
    rows_p = [[] for _ in range(7)]
    rows_s = [[] for _ in range(7)]
    for l in range(DEPTH):
        lp = {
            'norm_mix': norm_mix[l], 'w_in': w_in[l], 'b_forget': b_forget[l],
            'qn_fox': qn_fox[l], 'kn_fox': kn_fox[l], 'qn_moba': qn_moba[l], 'kn_moba': kn_moba[l],
            'rwkv_mu': rwkv_mu[l], 'rwkv_w0': rwkv_w0[l], 'rwkv_w_up': rwkv_w_up[l], 'rwkv_a0': rwkv_a0[l],
            'rwkv_a_up': rwkv_a_up[l], 'rwkv_g_up': rwkv_g_up[l], 'rwkv_k_k': rwkv_k_k[l], 'rwkv_k_a': rwkv_k_a[l],
            'rwkv_r_k': rwkv_r_k[l], 'rwkv_lnx_w': rwkv_lnx_w[l], 'rwkv_lnx_b': rwkv_lnx_b[l],
            'w_br_fox': w_br_fox[l], 'w_br_rwkv': w_br_rwkv[l], 'w_br_moba': w_br_moba[l], 'w_out': w_out[l],
            'norm_mlp': norm_mlp[l], 'w_mlp_up': w_mlp_up[l], 'w_mlp_down': w_mlp_down[l],
        }
        past = {
            'fox_k': cache_fox_k[page_table, l].reshape(db, past_len, H_FOX, HEAD_DIM),
            'fox_v': cache_fox_v[page_table, l].reshape(db, past_len, H_FOX, HEAD_DIM),
            'fox_logf': cache_fox_logf[page_table, l].reshape(db, past_len, H_FOX),
            'moba_k': cache_moba_k[page_table, l].reshape(db, past_len, H_MOBA, HEAD_DIM),
            'moba_v': cache_moba_v[page_table, l].reshape(db, past_len, H_MOBA, HEAD_DIM),
            'rwkv': state_rwkv[:, l],
            'rwkv_shift': state_rwkv_shift[:, l],
        }
        xp, new_p = layer_forward(xp, pos_p, None, lp)
        xs, new_s = layer_forward(xs, pos_s, past, lp)
        for acc, val in zip(rows_p, new_p):
            acc.append(val)
        for acc, val in zip(rows_s, new_s):
            acc.append(val)
    p_fox_k, p_fox_v, p_fox_logf, p_moba_k, p_moba_v, p_rwkv, p_rwkv_shift = [jnp.stack(a, axis=1) for a in rows_p]
    s_fox_k, s_fox_v, s_fox_logf, s_moba_k, s_moba_v, s_rwkv, s_rwkv_shift = [jnp.stack(a, axis=1) for a in rows_s]
    return (xp, xs, p_fox_k, p_fox_v, p_fox_logf, p_moba_k, p_moba_v, p_rwkv, p_rwkv_shift,
            s_fox_k, s_fox_v, s_fox_logf, s_moba_k, s_moba_v, s_rwkv, s_rwkv_shift)
```

```python
import functools
import math

import numpy as np
import jax
import jax.numpy as jnp
from jax import lax
from jax.experimental import pallas as pl
from jax.experimental.pallas import tpu as pltpu

F32 = jnp.float32
BF16 = jnp.bfloat16

HEAD_DIM = 64
N_HEADS = 8
W_HEADS = N_HEADS * HEAD_DIM
N_PAIRS = N_HEADS // 2
RWKV_COLS = 3 * W_HEADS + 64 + 64 + 128
N_BRANCH = 3
MOBA_BLOCK = 256
MOBA_TOPK = 3
ROPE_THETA = 500000.0
ROPE_DIM = HEAD_DIM // 4
NORM_EPS = 1e-6
LNX_EPS = 64e-5
L2_EPS = 1e-12
ATT_SCALE = HEAD_DIM ** -0.5

LANES = 128
SUBLANES = 8
VMEM_LIMIT = 56 * 1024 * 1024

NEG = -1e30
BIAS_OFF = -(2.0 ** 100)
RWKV_CHUNK = 64


def _cparams(sem):
    return pltpu.CompilerParams(dimension_semantics=sem, vmem_limit_bytes=VMEM_LIMIT)


def _dot(a, b):
    return jnp.dot(a, b, preferred_element_type=F32)


def _dot_nt(a, b):
    return lax.dot_general(a, b, (((1,), (1,)), ((), ())), preferred_element_type=F32)


def _split2(x):
    hi = x.astype(BF16)
    lo = (x - hi.astype(F32)).astype(BF16)
    return hi, lo


def _split3(x):
    hi = x.astype(BF16)
    r1 = x - hi.astype(F32)
    mid = r1.astype(BF16)
    lo = (r1 - mid.astype(F32)).astype(BF16)
    return hi, mid, lo


def _dot3(a, b, dot=_dot):
    ah, al = _split2(a)
    bh, bl = _split2(b)
    return dot(ah, bh) + (dot(ah, bl) + dot(al, bh))


def _dot_exact_lhs(a_bf16, b, dot=_dot):
    h, m, l = _split3(b)
    return dot(a_bf16, h) + (dot(a_bf16, m) + dot(a_bf16, l))


def _dot_exact_rhs(a, b_bf16, dot=_dot):
    h, m, l = _split3(a)
    return dot(h, b_bf16) + (dot(m, b_bf16) + dot(l, b_bf16))


def _head_reduce(x, g_bf16):
    hi, lo = _split2(x)
    return _dot(hi, g_bf16) + _dot(lo, g_bf16)


def _softplus(z):
    return jnp.maximum(z, 0.0) + jnp.log(1.0 + jnp.exp(-jnp.abs(z)))


def _sigmoid(z):
    return 1.0 / (1.0 + jnp.exp(-z))


def _rms(x, g):
    ms = jnp.mean(x * x, axis=-1, keepdims=True)
    return x * lax.rsqrt(ms + NORM_EPS) * g


def _iota(shape, dim):
    return lax.broadcasted_iota(jnp.int32, shape, dim)


def _proj_plain_kernel(x_ref, g_ref, w_ref, o_ref, *, act):
    h = _rms(x_ref[...], g_ref[...]).astype(BF16)
    u = _dot(h, w_ref[...])
    if act == "sigmoid":
        u = _sigmoid(u)
    o_ref[...] = u


def _proj_plain(x, g, w, act=None, tm=256):
    m, d = x.shape
    n = w.shape[1]
    return pl.pallas_call(
        functools.partial(_proj_plain_kernel, act=act),
        grid=(m // tm,),
        in_specs=[pl.BlockSpec((tm, d), lambda i: (i, 0)),
                  pl.BlockSpec((1, d), lambda i: (0, 0)),
                  pl.BlockSpec((d, n), lambda i: (0, 0))],
        out_specs=pl.BlockSpec((tm, n), lambda i: (i, 0)),
        out_shape=jax.ShapeDtypeStruct((m, n), F32),
        compiler_params=_cparams(("parallel",)),
    )(x, g, w)


def _proj_fox_kernel(x_ref, g_ref, w_ref, wf_ref, bf_ref, qn_ref, kn_ref, gm_ref,
                     q_ref, k_ref, v_ref, lf_ref, c_ref, carry_scr, *, tiles_per_seq):
    i = pl.program_id(0)
    tm = x_ref.shape[0]
    h = _rms(x_ref[...], g_ref[...]).astype(BF16)
    u = _dot(h, w_ref[...])
    gm = gm_ref[...]
    fq = u[:, 0:W_HEADS]
    fk = u[:, W_HEADS:2 * W_HEADS]
    q_ms = _head_reduce(fq * fq, gm) * (1.0 / HEAD_DIM)
    k_ms = _head_reduce(fk * fk, gm) * (1.0 / HEAD_DIM)
    q_ref[...] = (fq * lax.rsqrt(q_ms + NORM_EPS) * qn_ref[...] * ATT_SCALE).astype(BF16)
    k_ref[...] = fk * lax.rsqrt(k_ms + NORM_EPS) * kn_ref[...]
    v_ref[...] = u[:, 2 * W_HEADS:3 * W_HEADS]
    z = _dot(h, wf_ref[...]) + bf_ref[...]
    lane = _iota((1, LANES), 1)
    lf = jnp.where(lane < N_HEADS, -_softplus(-z), 0.0)
    lf_ref[...] = lf[:, 0:N_HEADS]

    @pl.when(i % tiles_per_seq == 0)
    def _():
        carry_scr[...] = jnp.zeros_like(carry_scr)

    tri = (_iota((tm, tm), 0) >= _iota((tm, tm), 1)).astype(BF16)
    c = _dot_exact_lhs(tri, lf) + carry_scr[...]
    carry_scr[...] = c[tm - 1:tm, :]
    c_ref[...] = c[:, 0:N_HEADS]


def _proj_fox(x, g, w, wf, bfg, qn, kn, gm, tiles_per_seq, tm=256):
    m, d = x.shape
    row = lambda i: (i, 0)
    fix = lambda i: (0, 0)
    return pl.pallas_call(
        functools.partial(_proj_fox_kernel, tiles_per_seq=tiles_per_seq),
        grid=(m // tm,),
        in_specs=[pl.BlockSpec((tm, d), row), pl.BlockSpec((1, d), fix),
                  pl.BlockSpec((d, 3 * W_HEADS), fix), pl.BlockSpec((d, LANES), fix),
                  pl.BlockSpec((1, LANES), fix), pl.BlockSpec((1, W_HEADS), fix),
                  pl.BlockSpec((1, W_HEADS), fix), pl.BlockSpec((W_HEADS, W_HEADS), fix)],
        out_specs=[pl.BlockSpec((tm, W_HEADS), row), pl.BlockSpec((tm, W_HEADS), row),
                   pl.BlockSpec((tm, W_HEADS), row), pl.BlockSpec((tm, N_HEADS), row),
                   pl.BlockSpec((tm, N_HEADS), row)],
        out_shape=[jax.ShapeDtypeStruct((m, W_HEADS), BF16),
                   jax.ShapeDtypeStruct((m, W_HEADS), F32),
                   jax.ShapeDtypeStruct((m, W_HEADS), F32),
                   jax.ShapeDtypeStruct((m, N_HEADS), F32),
                   jax.ShapeDtypeStruct((m, N_HEADS), F32)],
        scratch_shapes=[pltpu.VMEM((1, LANES), F32)],
        compiler_params=_cparams(("arbitrary",)),
    )(x, g, w, wf, bfg, qn, kn, gm)


def _rotary(x, cos, sp, sm):
    outs = []
    for j in range(W_HEADS // LANES):
        xs = x[:, j * LANES:(j + 1) * LANES]
        outs.append(xs * cos + pltpu.roll(xs, ROPE_DIM // 2, 1) * sp
                    + pltpu.roll(xs, LANES - ROPE_DIM // 2, 1) * sm)
    return jnp.concatenate(outs, axis=1)


def _proj_moba_kernel(x_ref, g_ref, w_ref, qn_ref, kn_ref, gm_ref, cos_ref, sp_ref, sm_ref,
                      q_ref, k_ref, v_ref):
    h = _rms(x_ref[...], g_ref[...]).astype(BF16)
    u = _dot(h, w_ref[...])
    gm = gm_ref[...]
    mq = u[:, 0:W_HEADS]
    mk = u[:, W_HEADS:2 * W_HEADS]
    q_ms = _head_reduce(mq * mq, gm) * (1.0 / HEAD_DIM)
    k_ms = _head_reduce(mk * mk, gm) * (1.0 / HEAD_DIM)
    cos, sp, sm = cos_ref[...], sp_ref[...], sm_ref[...]
    q_ref[...] = _rotary(mq * lax.rsqrt(q_ms + NORM_EPS) * qn_ref[...], cos, sp, sm)
    k_ref[...] = _rotary(mk * lax.rsqrt(k_ms + NORM_EPS) * kn_ref[...], cos, sp, sm)
    v_ref[...] = u[:, 2 * W_HEADS:3 * W_HEADS]


def _proj_moba(x, g, w, qn, kn, gm, cos, sp, sm, tm=256):
    m, d = x.shape
    ntab = cos.shape[0] // tm
    row = lambda i: (i, 0)
    fix = lambda i: (0, 0)
    tab = lambda i: (i % ntab, 0)
    return pl.pallas_call(
        _proj_moba_kernel,
        grid=(m // tm,),
        in_specs=[pl.BlockSpec((tm, d), row), pl.BlockSpec((1, d), fix),
                  pl.BlockSpec((d, 3 * W_HEADS), fix), pl.BlockSpec((1, W_HEADS), fix),
                  pl.BlockSpec((1, W_HEADS), fix), pl.BlockSpec((W_HEADS, W_HEADS), fix),
                  pl.BlockSpec((tm, LANES), tab), pl.BlockSpec((tm, LANES), tab),
                  pl.BlockSpec((tm, LANES), tab)],
        out_specs=[pl.BlockSpec((tm, W_HEADS), row)] * 3,
        out_shape=[jax.ShapeDtypeStruct((m, W_HEADS), F32)] * 3,
        compiler_params=_cparams(("parallel",)),
    )(x, g, w, qn, kn, gm, cos, sp, sm)


def _flash_kernel(qt_ref, kt_ref, *refs, mode, tq, tk):
    if mode == "fox":
        q_ref, k_ref, v_ref, ck_ref, o_ref, qs_scr, m_scr, l_scr, acc_scr = refs
    else:
        qe_ref, qo_ref, k_ref, v_ref, o_ref, qs_scr, m_scr, l_scr, acc_scr = refs
    p = pl.program_id(1)
    s_idx = pl.program_id(2)
    qi = qt_ref[s_idx]
    ki = kt_ref[s_idx]
    lane = _iota((1, LANES), 1)

    @pl.when(ki == 0)
    def _():
        m_scr[...] = jnp.full_like(m_scr, NEG)
        l_scr[...] = jnp.zeros_like(l_scr)
        acc_scr[...] = jnp.zeros_like(acc_scr)
        if mode == "fox":
            q = q_ref[0]
            qs_scr[0] = jnp.where(lane < HEAD_DIM, q, jnp.zeros_like(q))
            qs_scr[1] = jnp.where(lane >= HEAD_DIM, q, jnp.zeros_like(q))
        else:
            qs_scr[0] = qe_ref[0]
            qs_scr[1] = qo_ref[0]

    k = k_ref[0]
    v = v_ref[0].astype(BF16)
    causal = (ki * tk + _iota((tq, tk), 1)) <= (qi * tq + _iota((tq, tk), 0))
    if mode == "moba":
        key_blk = (ki * tk + _iota((tk, LANES), 0)) // MOBA_BLOCK
        lane2 = _iota((tk, LANES), 1)
    for h in range(2):
        if mode == "fox":
            kh = k.astype(BF16)
        else:
            base = HEAD_DIM if h == 0 else 0
            other = (lane2 >= HEAD_DIM) if h == 0 else (lane2 < HEAD_DIM)
            onehot = jnp.where(lane2 - base == key_blk, 1.0, 0.0)
            kh = jnp.where(other, onehot, k).astype(BF16)
        s = _dot_nt(qs_scr[h], kh)
        if mode == "fox":
            s = s - ck_ref[0, pl.ds(2 * p + h, 1), :]
        s = jnp.where(causal, s, NEG)
        m_prev = m_scr[h]
        m_new = jnp.maximum(m_prev, jnp.max(s, axis=-1, keepdims=True))
        alpha = jnp.exp(m_prev - m_new)
        pe = jnp.exp(s - m_new)
        l_scr[h] = alpha * l_scr[h] + jnp.sum(pe, axis=-1, keepdims=True)
        acc_scr[h] = alpha * acc_scr[h] + _dot(pe.astype(BF16), v)
        m_scr[h] = m_new

    @pl.when(ki == qi)
    def _():
        o0 = acc_scr[0] / l_scr[0]
        o1 = acc_scr[1] / l_scr[1]
        o_ref[0] = jnp.where(lane < HEAD_DIM, o0, o1)


def _tri_tables(nq):
    qt = np.concatenate([np.full(i + 1, i, np.int32) for i in range(nq)])
    kt = np.concatenate([np.arange(i + 1, dtype=np.int32) for i in range(nq)])
    return jnp.asarray(qt), jnp.asarray(kt)


def _flash(mode, qs, k, v, ck_t, tile):
    b, t, _ = k.shape
    tq = tk = tile
    nq = t // tq
    qt, kt = _tri_tables(nq)
    qmap = lambda bb, p, s, qt_, kt_: (bb, qt_[s], p)
    kmap = lambda bb, p, s, qt_, kt_: (bb, kt_[s], p)
    in_specs = [pl.BlockSpec((1, tq, LANES), qmap) for _ in qs]
    in_specs += [pl.BlockSpec((1, tk, LANES), kmap), pl.BlockSpec((1, tk, LANES), kmap)]
    args = list(qs) + [k, v]
    if mode == "fox":
        in_specs.append(pl.BlockSpec((1, N_HEADS, tk), lambda bb, p, s, qt_, kt_: (bb, 0, kt_[s])))
        args.append(ck_t)
    grid_spec = pltpu.PrefetchScalarGridSpec(
        num_scalar_prefetch=2, grid=(b, N_PAIRS, int(qt.shape[0])),
        in_specs=in_specs,
        out_specs=pl.BlockSpec((1, tq, LANES), qmap),
        scratch_shapes=[pltpu.VMEM((2, tq, LANES), BF16), pltpu.VMEM((2, tq, 1), F32),
                        pltpu.VMEM((2, tq, 1), F32), pltpu.VMEM((2, tq, LANES), F32)])
    return pl.pallas_call(
        functools.partial(_flash_kernel, mode=mode, tq=tq, tk=tk),
        grid_spec=grid_spec,
        out_shape=jax.ShapeDtypeStruct((b, t, W_HEADS), F32),
        compiler_params=_cparams(("parallel", "parallel", "arbitrary")),
    )(qt, kt, *args)


def _top3_select(gm, n_idx):
    sel = jnp.zeros(gm.shape, jnp.bool_)
    big = jnp.int32(1 << 20)
    for _ in range(MOBA_TOPK):
        mx = jnp.max(gm, axis=-1, keepdims=True)
        is_max = jnp.logical_and(gm == mx, mx > 0.5 * NEG)
        idx = jnp.min(jnp.where(is_max, n_idx, big), axis=-1, keepdims=True)
        pick = n_idx == idx
        sel = jnp.logical_or(sel, pick)
        gm = jnp.where(pick, NEG, gm)
    return sel


def _moba_gate_kernel(q_ref, k_ref, qe_ref, qo_ref, kme_scr, kmo_scr, *, n_blk):
    i = pl.program_id(1)
    rows = q_ref.shape[1]

    @pl.when(i == 0)
    def _():
        kme_scr[...] = jnp.zeros_like(kme_scr)
        kmo_scr[...] = jnp.zeros_like(kmo_scr)

    q = q_ref[0]
    lane512 = _iota((1, W_HEADS), 1)
    lane = _iota((1, LANES), 1)
    for p in range(N_PAIRS):
        qpair = q[:, p * LANES:(p + 1) * LANES] * ATT_SCALE
        for h in range(2):
            head = 2 * p + h
            qm = jnp.where(lane512 // HEAD_DIM == head, q, 0.0)
            km = kme_scr[...] if h == 0 else kmo_scr[...]
            g = _dot3(qm, km, _dot_nt)
            base = HEAD_DIM if h == 0 else 0
            n_idx = lane - base
            in_half = jnp.logical_and(n_idx >= 0, n_idx < HEAD_DIM)
            valid = jnp.logical_and(n_idx >= 0, n_idx < i)
            sel = _top3_select(jnp.where(valid, g, NEG), n_idx)
            keep = jnp.logical_or(jnp.logical_or(sel, n_idx == i), n_idx >= n_blk)
            bias = jnp.where(keep, 0.0, BIAS_OFF)
            qa = jnp.where(in_half, bias, qpair).astype(BF16)
            if h == 0:
                qe_ref[0, :, p * LANES:(p + 1) * LANES] = qa
            else:
                qo_ref[0, :, p * LANES:(p + 1) * LANES] = qa

    kmean = jnp.sum(k_ref[0], axis=0, keepdims=True) * (1.0 / rows)
    kme_scr[pl.ds(HEAD_DIM + i, 1), :] = kmean
    kmo_scr[pl.ds(i, 1), :] = kmean


def _moba_gate(q, k):
    b, t, _ = q.shape
    n_blk = t // MOBA_BLOCK
    assert n_blk <= HEAD_DIM
    blk = lambda bb, i: (bb, i, 0)
    return pl.pallas_call(
        functools.partial(_moba_gate_kernel, n_blk=n_blk),
        grid=(b, n_blk),
        in_specs=[pl.BlockSpec((1, MOBA_BLOCK, W_HEADS), blk)] * 2,
        out_specs=[pl.BlockSpec((1, MOBA_BLOCK, W_HEADS), blk)] * 2,
        out_shape=[jax.ShapeDtypeStruct((b, t, W_HEADS), BF16)] * 2,
        scratch_shapes=[pltpu.VMEM((LANES, W_HEADS), F32)] * 2,
        compiler_params=_cparams(("parallel", "arbitrary")),
    )(q, k)


def _rwkv_prep_kernel(u_ref, up_ref, mu_ref, w0_ref, wup_ref, a0_ref, aup_ref, gup_ref,
                      kk_ref_w, ka_ref, rk_ref, gs_ref,
                      r_o, ld_o, k_o, v_o, kk_o, b_o, bonus_o, g_o):
    u = u_ref[...]
    um = u + (up_ref[...] - u) * mu_ref[...]
    r = um[:, 0:W_HEADS]
    k = um[:, W_HEADS:2 * W_HEADS]
    v = um[:, 2 * W_HEADS:3 * W_HEADS]
    wa = um[:, 3 * W_HEADS:3 * W_HEADS + LANES]
    gd = um[:, 3 * W_HEADS + LANES:3 * W_HEADS + 2 * LANES]
    w_log = -_softplus(-(w0_ref[...] + _dot(jnp.tanh(wa).astype(BF16), wup_ref[...]))) - 0.5
    ld_o[...] = -jnp.exp(w_log)
    a = _sigmoid(a0_ref[...] + _dot(wa.astype(BF16), aup_ref[...]))
    g_o[...] = _dot(_sigmoid(gd).astype(BF16), gup_ref[...])
    gs = gs_ref[...]
    kx = k * kk_ref_w[...]
    kk = kx * lax.rsqrt(_head_reduce(kx * kx, gs) + L2_EPS)
    k2 = k * (1.0 + (a - 1.0) * ka_ref[...])
    r_o[...] = r
    k_o[...] = k2
    v_o[...] = v
    kk_o[...] = kk
    b_o[...] = kk * a
    bonus_o[...] = _head_reduce(r * k2 * rk_ref[...], gs) * v


def _rwkv_prep(u, up, mu, w0, wup, a0, aup, gup, k_k, k_a, r_k, gs, tm=256):
    m = u.shape[0]
    row = lambda i: (i, 0)
    fix = lambda i: (0, 0)
    vec = pl.BlockSpec((1, W_HEADS), fix)
    lr = pl.BlockSpec((LANES, W_HEADS), fix)
    return pl.pallas_call(
        _rwkv_prep_kernel,
        grid=(m // tm,),
        in_specs=[pl.BlockSpec((tm, RWKV_COLS), row), pl.BlockSpec((tm, RWKV_COLS), row),
                  pl.BlockSpec((1, RWKV_COLS), fix), vec, lr, vec, lr, lr, vec, vec, vec,
                  pl.BlockSpec((W_HEADS, W_HEADS), fix)],
        out_specs=[pl.BlockSpec((tm, W_HEADS), row)] * 8,
        out_shape=[jax.ShapeDtypeStruct((m, W_HEADS), F32)] * 8,
        compiler_params=_cparams(("parallel",)),
    )(u, up, mu, w0, wup, a0, aup, gup, k_k, k_a, r_k, gs)


def _rwkv_scan_kernel(r_ref, ld_ref, k_ref, v_ref, kk_ref, b_ref, s0_ref, y_ref, sT_ref, st_scr,
                      *, chunk, n_double):
    c = pl.program_id(2)
    cc = chunk

    @pl.when(c == 0)
    def _():
        st_scr[...] = s0_ref[0, 0]

    r, ld, k, v, kk, b = (x[0] for x in (r_ref, ld_ref, k_ref, v_ref, kk_ref, b_ref))
    tri = (_iota((cc, cc), 0) >= _iota((cc, cc), 1)).astype(BF16)
    cum = _dot_exact_lhs(tri, ld)
    cum_last = cum[cc - 1:cc, :]
    g_in = jnp.exp(cum)
    g_inv = jnp.exp(-cum)
    g_end = jnp.exp(cum_last - cum)
    al = -kk * jnp.exp(cum - ld)
    m0 = _iota((cc, LANES), 1) < HEAD_DIM

    def stack(x):
        return jnp.concatenate([jnp.where(m0, x, 0.0), jnp.where(m0, 0.0, x)], axis=0)

    a2, r2 = stack(al), stack(r * g_in)
    b2, k2 = stack(b * g_inv), stack(k * g_inv)
    v2 = stack(v)
    bg2, kg2 = stack(b * g_end), stack(k * g_end)
    ar = jnp.concatenate([a2, r2], axis=0)
    mb = _dot3(ar, b2, _dot_nt)
    mk = _dot3(ar, k2, _dot_nt)
    rr = _iota((2 * cc, 2 * cc), 0)
    cl = _iota((2 * cc, 2 * cc), 1)
    same = (rr // cc) == (cl // cc)
    strict = jnp.logical_and(same, (rr % cc) > (cl % cc))
    incl = jnp.logical_and(same, (rr % cc) >= (cl % cc))
    m_ab = jnp.where(strict, mb[0:2 * cc], 0.0)
    m_ak = jnp.where(strict, mk[0:2 * cc], 0.0)
    n_rb = jnp.where(incl, mb[2 * cc:4 * cc], 0.0)
    n_rk = jnp.where(incl, mk[2 * cc:4 * cc], 0.0)
    t_inv = jnp.where(rr == cl, 1.0, 0.0) + m_ab
    pw = m_ab
    for _ in range(n_double - 1):
        pw = _dot3(pw, pw)
        t_inv = t_inv + _dot3(pw, t_inv)
    st = st_scr[...]
    ars = _dot3(ar, st)
    z2 = _dot3(t_inv, ars[0:2 * cc] + _dot3(m_ak, v2))
    y2 = ars[2 * cc:4 * cc] + _dot3(n_rb, z2) + _dot3(n_rk, v2)
    y_ref[0] = y2[0:cc] + y2[cc:2 * cc]
    g_col = jnp.transpose(jnp.broadcast_to(g_in[cc - 1:cc, :], (LANES, LANES)))
    st_new = st * g_col + _dot3(jnp.transpose(bg2), z2) + _dot3(jnp.transpose(kg2), v2)
    st_scr[...] = st_new

    @pl.when(c == pl.num_programs(2) - 1)
    def _():
        sT_ref[0, 0] = st_new


def _rwkv_scan(r, ld, k, v, kk, b, s0_bd):
    bsz, t, _ = r.shape
    cc = RWKV_CHUNK
    assert t % cc == 0
    blk = lambda bb, p, c: (bb, c, p)
    st = lambda bb, p, c: (bb, p, 0, 0)
    return pl.pallas_call(
        functools.partial(_rwkv_scan_kernel, chunk=cc, n_double=int(math.log2(cc))),
        grid=(bsz, N_PAIRS, t // cc),
        in_specs=[pl.BlockSpec((1, cc, LANES), blk)] * 6 + [pl.BlockSpec((1, 1, LANES, LANES), st)],
        out_specs=[pl.BlockSpec((1, cc, LANES), blk), pl.BlockSpec((1, 1, LANES, LANES), st)],
        out_shape=[jax.ShapeDtypeStruct((bsz, t, W_HEADS), F32),
                   jax.ShapeDtypeStruct((bsz, N_PAIRS, LANES, LANES), F32)],
        scratch_shapes=[pltpu.VMEM((LANES, LANES), F32)],
        compiler_params=_cparams(("parallel", "parallel", "arbitrary")),
    )(r, ld, k, v, kk, b, s0_bd)


def _merge_kernel(x_ref, yf_ref, ym_ref, yr_ref, bonus_ref, g_ref, gates_ref,
                  lw_ref, lb_ref, gm_ref, wf_ref, wr_ref, wm_ref, wo_ref, o_ref):
    d = x_ref.shape[1]
    gm = gm_ref[...]
    yr = yr_ref[...]
    mean = _head_reduce(yr, gm) * (1.0 / HEAD_DIM)
    dv = yr - mean
    var = _head_reduce(dv * dv, gm) * (1.0 / HEAD_DIM)
    yn = dv * lax.rsqrt(var + LNX_EPS) * lw_ref[...] + lb_ref[...]
    y_rwkv = (yn + bonus_ref[...]) * g_ref[...]
    gates = gates_ref[...]
    merged = (gates[:, 0:d] * _dot(yf_ref[...].astype(BF16), wf_ref[...])
              + gates[:, d:2 * d] * _dot(y_rwkv.astype(BF16), wr_ref[...])
              + gates[:, 2 * d:3 * d] * _dot(ym_ref[...].astype(BF16), wm_ref[...]))
    o_ref[...] = x_ref[...] + _dot(merged.astype(BF16), wo_ref[...])


def _merge(x, yf, ym, yr, bonus, g, gates, lw, lb, gm, wf, wr, wm, wo, tm=256):
    m, d = x.shape
    row = lambda i: (i, 0)
    fix = lambda i: (0, 0)
    act = pl.BlockSpec((tm, W_HEADS), row)
    vec = pl.BlockSpec((1, W_HEADS), fix)
    wbr = pl.BlockSpec((W_HEADS, d), fix)
    return pl.pallas_call(
        _merge_kernel,
        grid=(m // tm,),
        in_specs=[pl.BlockSpec((tm, d), row), act, act, act, act, act,
                  pl.BlockSpec((tm, N_BRANCH * d), row), vec, vec,
                  pl.BlockSpec((W_HEADS, W_HEADS), fix), wbr, wbr, wbr, pl.BlockSpec((d, d), fix)],
        out_specs=pl.BlockSpec((tm, d), row),
        out_shape=jax.ShapeDtypeStruct((m, d), F32),
        compiler_params=_cparams(("parallel",)),
    )(x, yf, ym, yr, bonus, g, gates, lw, lb, gm, wf, wr, wm, wo)


def _mlp_kernel(x_ref, g_ref, wu_ref, wd_ref, o_ref, h_scr, acc_scr):
    f = pl.program_id(1)

    @pl.when(f == 0)
    def _():
        h_scr[...] = _rms(x_ref[...], g_ref[...]).astype(BF16)
        acc_scr[...] = x_ref[...]

    a = jnp.maximum(_dot(h_scr[...], wu_ref[...]), 0.0)
    acc_scr[...] += _dot((a * a).astype(BF16), wd_ref[...])

    @pl.when(f == pl.num_programs(1) - 1)
    def _():
        o_ref[...] = acc_scr[...]


def _mlp(x, g, wu, wd, tm=256, tf=2048):
    m, d = x.shape
    dff = wu.shape[1]
    return pl.pallas_call(
        _mlp_kernel,
        grid=(m // tm, dff // tf),
        in_specs=[pl.BlockSpec((tm, d), lambda i, f: (i, 0)), pl.BlockSpec((1, d), lambda i, f: (0, 0)),
                  pl.BlockSpec((d, tf), lambda i, f: (0, f)), pl.BlockSpec((tf, d), lambda i, f: (f, 0))],
        out_specs=pl.BlockSpec((tm, d), lambda i, f: (i, 0)),
        out_shape=jax.ShapeDtypeStruct((m, d), F32),
        scratch_shapes=[pltpu.VMEM((tm, d), BF16), pltpu.VMEM((tm, d), F32)],
        compiler_params=_cparams(("parallel", "arbitrary")),
    )(x, g, wu, wd)


def _lane_prefix(x, step, width):
    lane = _iota(x.shape, 1)
    sh = step
    while sh < width:
        x = x + jnp.where(lane >= sh, pltpu.roll(x, sh, 1), 0.0)
        sh *= 2
    return x


def _lane_periodic(x, step, width):
    sh = step
    while sh < width:
        x = x + pltpu.roll(x, sh, 1)
        sh *= 2
    return x


def _fox_dec_kernel(pt_ref, q_ref, kn_ref, vn_ref, lfn_ref, *refs, n_pg, page):
    k_refs = refs[0:n_pg]
    v_refs = refs[n_pg:2 * n_pg]
    lf_refs = refs[2 * n_pg:3 * n_pg]
    o_ref, m_scr, l_scr, acc_scr, carry_scr = refs[3 * n_pg:]
    j = pl.program_id(1)
    width = page * N_HEADS
    rows = q_ref.shape[1]

    @pl.when(j == 0)
    def _():
        m_scr[...] = jnp.full_like(m_scr, NEG)
        l_scr[...] = jnp.zeros_like(l_scr)
        acc_scr[...] = jnp.zeros_like(acc_scr)
        carry_scr[...] = jnp.zeros_like(carry_scr)

    q = q_ref[0]
    lane = _iota((1, width), 1)

    def cum_rows(x, carry):
        n = x.shape[0]
        loc = _lane_prefix(x, N_HEADS, width)
        tot = _lane_periodic(pltpu.roll(jnp.where(lane >= width - N_HEADS, loc, 0.0), N_HEADS, 1),
                             N_HEADS, width)
        run = tot
        sh = 1
        rowi = _iota(tot.shape, 0)
        while sh < n:
            run = run + jnp.where(rowi >= sh, pltpu.roll(run, sh, 0), 0.0)
            sh *= 2
        cum = loc + (run - tot) + carry
        return cum, run[n - 1:n, :] + carry

    lf = jnp.concatenate([r[0, 0] for r in lf_refs], axis=0)
    if n_pg < SUBLANES:
        lf = jnp.concatenate([lf, jnp.zeros((SUBLANES - n_pg, width), F32)], axis=0)
    ck, carry = cum_rows(lf, carry_scr[...])
    carry_scr[...] = carry

    def attend(s, vmat):
        m_prev = m_scr[...]
        m_new = jnp.maximum(m_prev, jnp.max(s, axis=-1, keepdims=True))
        alpha = jnp.exp(m_prev - m_new)
        pe = jnp.exp(s - m_new)
        l_scr[...] = alpha * l_scr[...] + jnp.sum(pe, axis=-1, keepdims=True)
        acc_scr[...] = alpha * acc_scr[...] + _dot(pe.astype(BF16), vmat)
        m_scr[...] = m_new

    same_head = (_iota((rows, width), 0) % N_HEADS) == (_iota((rows, width), 1) % N_HEADS)
    for i in range(n_pg):
        kx = k_refs[i][0, 0].reshape(width, HEAD_DIM).astype(BF16)
        vx = v_refs[i][0, 0].reshape(width, HEAD_DIM).astype(BF16)
        s = _dot_nt(q, kx) - ck[i:i + 1, :]
        attend(jnp.where(same_head, s, NEG), vx)

    @pl.when(j == pl.num_programs(1) - 1)
    def _():
        lfn = jnp.concatenate([lfn_ref[0], jnp.zeros((SUBLANES - 1, width), F32)], axis=0)
        cn, _ = cum_rows(lfn, carry_scr[...])
        nn = kn_ref.shape[1]
        sn = _dot_nt(q, kn_ref[0].astype(BF16)) - cn[0:1, 0:nn]
        rq = _iota((rows, nn), 0)
        ck_ = _iota((rows, nn), 1)
        ok = jnp.logical_and(rq % N_HEADS == ck_ % N_HEADS, ck_ // N_HEADS <= rq // N_HEADS)
        attend(jnp.where(ok, sn, NEG), vn_ref[0].astype(BF16))
        o_ref[0] = acc_scr[...] / l_scr[...]


def _fox_decode(page_table, q, kn, vn, lfn, cache_k, cache_v, cache_lf, layer, n_pg=8):
    bsz, n_pages = page_table.shape
    page = cache_k.shape[2]
    width = page * N_HEADS
    rows = q.shape[1]
    assert n_pages % n_pg == 0 and n_pg <= SUBLANES
    seq = lambda bb, j, pt: (bb, 0, 0)

    def pspec(i, shape):
        nz = (0,) * (len(shape) - 2)
        return pl.BlockSpec(shape, lambda bb, j, pt, i=i: (pt[bb, j * n_pg + i], layer) + nz)

    kv_shape = (1, 1, page, N_HEADS, HEAD_DIM)
    in_specs = [pl.BlockSpec((1, rows, HEAD_DIM), seq), pl.BlockSpec((1, kn.shape[1], HEAD_DIM), seq),
                pl.BlockSpec((1, vn.shape[1], HEAD_DIM), seq), pl.BlockSpec((1, 1, width), seq)]
    in_specs += [pspec(i, kv_shape) for i in range(n_pg)]
    in_specs += [pspec(i, kv_shape) for i in range(n_pg)]
    in_specs += [pspec(i, (1, 1, 1, width)) for i in range(n_pg)]
    grid_spec = pltpu.PrefetchScalarGridSpec(
        num_scalar_prefetch=1, grid=(bsz, n_pages // n_pg), in_specs=in_specs,
        out_specs=pl.BlockSpec((1, rows, HEAD_DIM), seq),
        scratch_shapes=[pltpu.VMEM((rows, 1), F32), pltpu.VMEM((rows, 1), F32),
                        pltpu.VMEM((rows, HEAD_DIM), F32), pltpu.VMEM((1, width), F32)])
    return pl.pallas_call(
        functools.partial(_fox_dec_kernel, n_pg=n_pg, page=page),
        grid_spec=grid_spec,
        out_shape=jax.ShapeDtypeStruct((bsz, rows, HEAD_DIM), F32),
        compiler_params=_cparams(("parallel", "arbitrary")),
    )(page_table, q, kn, vn, lfn, *([cache_k] * n_pg), *([cache_v] * n_pg), *([cache_lf] * n_pg))


def _moba_dec_kernel(pt_ref, q_ref, kn_ref, vn_ref, *refs, n_pg, page, n_blk):
    k_refs = refs[0:n_pg]
    v_refs = refs[n_pg:2 * n_pg]
    o_ref, g_scr, m_scr, l_scr, o_scr = refs[2 * n_pg:]
    n = pl.program_id(1)
    width = page * N_HEADS
    rows = q_ref.shape[1]
    lane = _iota((1, LANES), 1)

    @pl.when(n == 0)
    def _():
        g_scr[...] = jnp.zeros_like(g_scr)
        m_scr[...] = jnp.full_like(m_scr, NEG)
        l_scr[...] = jnp.zeros_like(l_scr)

    q = q_ref[0]
    qs = (q * ATT_SCALE).astype(BF16)
    kx = jnp.concatenate([r[0, 0].reshape(width, HEAD_DIM) for r in k_refs], axis=0)
    vx = jnp.concatenate([r[0, 0].reshape(width, HEAD_DIM) for r in v_refs], axis=0).astype(BF16)
    ntok = n_pg * page
    kmean = jnp.sum(kx.reshape(ntok, N_HEADS, HEAD_DIM), axis=0) * (1.0 / ntok)
    kmt = jnp.concatenate([kmean] * (rows // N_HEADS), axis=0)
    gate = jnp.sum(q * kmt, axis=-1, keepdims=True)
    onehot = lane == n
    g_scr[...] = jnp.where(onehot, gate, g_scr[...])

    same_head = (_iota((rows, n_pg * width), 0) % N_HEADS) == (_iota((rows, n_pg * width), 1) % N_HEADS)
    s = jnp.where(same_head, _dot_nt(qs, kx.astype(BF16)), NEG)
    m_n = jnp.max(s, axis=-1, keepdims=True)
    pe = jnp.exp(s - m_n)
    m_scr[...] = jnp.where(onehot, m_n, m_scr[...])
    l_scr[...] = jnp.where(onehot, jnp.sum(pe, axis=-1, keepdims=True), l_scr[...])
    o_scr[n] = _dot(pe.astype(BF16), vx)

    @pl.when(n == n_blk - 1)
    def _():
        sel = _top3_select(jnp.where(lane < n_blk, g_scr[...], NEG), lane + jnp.zeros((rows, LANES), jnp.int32))
        nn = kn_ref.shape[1]
        sn = _dot_nt(qs, kn_ref[0].astype(BF16))
        rq = _iota((rows, nn), 0)
        ck_ = _iota((rows, nn), 1)
        ok = jnp.logical_and(rq % N_HEADS == ck_ % N_HEADS, ck_ // N_HEADS <= rq // N_HEADS)
        sn = jnp.where(ok, sn, NEG)
        m_o = jnp.max(sn, axis=-1, keepdims=True)
        pn = jnp.exp(sn - m_o)
        l_o = jnp.sum(pn, axis=-1, keepdims=True)
        o_o = _dot(pn.astype(BF16), vn_ref[0].astype(BF16))
        m_all = m_scr[...]
        m_tot = jnp.maximum(m_o, jnp.max(jnp.where(sel, m_all, NEG), axis=-1, keepdims=True))
        w = jnp.where(sel, jnp.exp(m_all - m_tot), 0.0)
        e_o = jnp.exp(m_o - m_tot)
        l_tot = l_o * e_o + jnp.sum(w * l_scr[...], axis=-1, keepdims=True)
        out = o_o * e_o
        for b_i in range(n_blk):
            out = out + w[:, b_i:b_i + 1] * o_scr[b_i]
        o_ref[0] = out / l_tot


def _moba_decode(page_table, q, kn, vn, cache_k, cache_v, layer):
    bsz, n_pages = page_table.shape
    page = cache_k.shape[2]
    assert MOBA_BLOCK % page == 0
    n_pg = MOBA_BLOCK // page
    assert n_pages % n_pg == 0
    n_blk = n_pages // n_pg
    assert n_blk <= LANES
    rows = q.shape[1]
    seq = lambda bb, j, pt: (bb, 0, 0)

    def pspec(i):
        return pl.BlockSpec((1, 1, page, N_HEADS, HEAD_DIM),
                            lambda bb, j, pt, i=i: (pt[bb, j * n_pg + i], layer, 0, 0, 0))

    in_specs = [pl.BlockSpec((1, rows, HEAD_DIM), seq), pl.BlockSpec((1, kn.shape[1], HEAD_DIM), seq),
                pl.BlockSpec((1, vn.shape[1], HEAD_DIM), seq)]
    in_specs += [pspec(i) for i in range(n_pg)] * 2
    grid_spec = pltpu.PrefetchScalarGridSpec(
        num_scalar_prefetch=1, grid=(bsz, n_blk), in_specs=in_specs,
        out_specs=pl.BlockSpec((1, rows, HEAD_DIM), seq),
        scratch_shapes=[pltpu.VMEM((rows, LANES), F32), pltpu.VMEM((rows, LANES), F32),
                        pltpu.VMEM((rows, LANES), F32), pltpu.VMEM((n_blk, rows, HEAD_DIM), F32)])
    return pl.pallas_call(
        functools.partial(_moba_dec_kernel, n_pg=n_pg, page=page, n_blk=n_blk),
        grid_spec=grid_spec,
        out_shape=jax.ShapeDtypeStruct((bsz, rows, HEAD_DIM), F32),
        compiler_params=_cparams(("parallel", "arbitrary")),
    )(page_table, q, kn, vn, *([cache_k] * n_pg), *([cache_v] * n_pg))


def _rope_tables(pos):
    half = ROPE_DIM // 2
    inv = ROPE_THETA ** (-jnp.arange(half, dtype=F32) / half)
    ang = pos.astype(F32)[:, None] * inv[None, :]
    cos, sin = jnp.cos(ang), jnp.sin(ang)
    t = pos.shape[0]
    one = jnp.ones((t, HEAD_DIM - ROPE_DIM), F32)
    zero = jnp.zeros((t, HEAD_DIM - ROPE_DIM), F32)
    z8 = jnp.zeros((t, half), F32)
    cos_h = jnp.concatenate([cos, cos, one], axis=1)
    sp_h = jnp.concatenate([z8, sin, zero], axis=1)
    sm_h = jnp.concatenate([-sin, z8, zero], axis=1)
    tile2 = lambda a: jnp.concatenate([a, a], axis=1)
    return tile2(cos_h), tile2(sp_h), tile2(sm_h)


def _layer_params(l, p):
    d = p["w_in"].shape[1]
    w_in = p["w_in"][l]
    o = 0
    w_fox = w_in[:, o:o + 3 * W_HEADS]; o += 3 * W_HEADS
    w_ff = w_in[:, o:o + N_HEADS]; o += N_HEADS
    w_moba = w_in[:, o:o + 3 * W_HEADS]; o += 3 * W_HEADS
    w_ur = w_in[:, o:o + RWKV_COLS]; o += RWKV_COLS
    w_ug = w_in[:, o:o + N_BRANCH * d]
    pad_l = lambda a, n: jnp.pad(a, ((0, 0), (0, n - a.shape[1])))
    tile_h = lambda a: jnp.tile(a, N_HEADS)[None, :]
    zeros64 = jnp.zeros((64, W_HEADS), F32)
    return dict(
        norm_mix=p["norm_mix"][l][None, :],
        w_fox=w_fox.astype(BF16), w_ff=pad_l(w_ff, LANES).astype(BF16),
        b_ff=pad_l(p["b_forget"][l][None, :], LANES),
        w_moba=w_moba.astype(BF16), w_ur=w_ur.astype(BF16), w_ug=w_ug.astype(BF16),
        qn_fox=tile_h(p["qn_fox"][l]), kn_fox=tile_h(p["kn_fox"][l]),
        qn_moba=tile_h(p["qn_moba"][l]), kn_moba=tile_h(p["kn_moba"][l]),
        mu=p["rwkv_mu"][l][None, :], w0=p["rwkv_w0"][l][None, :], a0=p["rwkv_a0"][l][None, :],
        w_up=jnp.concatenate([p["rwkv_w_up"][l], zeros64], axis=0).astype(BF16),
        a_up=jnp.concatenate([zeros64, p["rwkv_a_up"][l]], axis=0).astype(BF16),
        g_up=p["rwkv_g_up"][l].astype(BF16),
        k_k=p["rwkv_k_k"][l][None, :], k_a=p["rwkv_k_a"][l][None, :],
        r_k=p["rwkv_r_k"][l].reshape(1, W_HEADS),
        lnx_w=p["rwkv_lnx_w"][l][None, :], lnx_b=p["rwkv_lnx_b"][l][None, :],
        w_br_fox=p["w_br_fox"][l].astype(BF16), w_br_rwkv=p["w_br_rwkv"][l].astype(BF16),
        w_br_moba=p["w_br_moba"][l].astype(BF16), w_out=p["w_out"][l].astype(BF16),
        norm_mlp=p["norm_mlp"][l][None, :],
        w_up_mlp=p["w_mlp_up"][l].astype(BF16), w_down_mlp=p["w_mlp_down"][l].astype(BF16),
    )


def _state_to_blockdiag(s):
    b = s.shape[0]
    st = jnp.swapaxes(s, -1, -2).reshape(b, N_PAIRS, 2, HEAD_DIM, HEAD_DIM)
    z = jnp.zeros_like(st[:, :, 0])
    top = jnp.concatenate([st[:, :, 0], z], axis=-1)
    bot = jnp.concatenate([z, st[:, :, 1]], axis=-1)
    return jnp.concatenate([top, bot], axis=-2)


def _blockdiag_to_state(sbd):
    b = sbd.shape[0]
    s0 = sbd[:, :, 0:HEAD_DIM, 0:HEAD_DIM]
    s1 = sbd[:, :, HEAD_DIM:, HEAD_DIM:]
    st = jnp.stack([s0, s1], axis=2).reshape(b, N_HEADS, HEAD_DIM, HEAD_DIM)
    return jnp.swapaxes(st, -1, -2)


def _mixers_common(x2, lp, gm_mean, gs_ones, rope, shift0, s0, bsz, t, tiles_per_seq):
    m = x2.shape[0]
    fq, fk, fv, logf, c = _proj_fox(x2, lp["norm_mix"], lp["w_fox"], lp["w_ff"], lp["b_ff"],
                                    lp["qn_fox"], lp["kn_fox"], gm_mean, tiles_per_seq)
    mq, mk, mv = _proj_moba(x2, lp["norm_mix"], lp["w_moba"], lp["qn_moba"], lp["kn_moba"],
                            gm_mean, *rope)
    ur = _proj_plain(x2, lp["norm_mix"], lp["w_ur"])
    gates = _proj_plain(x2, lp["norm_mix"], lp["w_ug"], act="sigmoid")
    ur3 = ur.reshape(bsz, t, RWKV_COLS)
    u_prev = jnp.concatenate([shift0[:, None, :], ur3[:, :-1]], axis=1).reshape(m, RWKV_COLS)
    r, ld, k2, v, kk, b, bonus, g = _rwkv_prep(
        ur, u_prev, lp["mu"], lp["w0"], lp["w_up"], lp["a0"], lp["a_up"], lp["g_up"],
        lp["k_k"], lp["k_a"], lp["r_k"], gs_ones)
    t_pad = -(-t // RWKV_CHUNK) * RWKV_CHUNK
    seqs = [a.reshape(bsz, t, W_HEADS) for a in (r, ld, k2, v, kk, b)]
    if t_pad != t:
        seqs = [jnp.pad(a, ((0, 0), (0, t_pad - t), (0, 0))) for a in seqs]
    y_r, s_bd = _rwkv_scan(*seqs, _state_to_blockdiag(s0))
    y_r = y_r[:, :t].reshape(m, W_HEADS)
    return dict(fq=fq, fk=fk, fv=fv, logf=logf, c=c, mq=mq, mk=mk, mv=mv, gates=gates,
                y_r=y_r, bonus=bonus, g=g, s_new=_blockdiag_to_state(s_bd), shift_new=ur3[:, -1])


def _finish_layer(x2, lp, gm_mean, pc, y_fox, y_moba):
    xo = _merge(x2, y_fox, y_moba, pc["y_r"], pc["bonus"], pc["g"], pc["gates"],
                lp["lnx_w"], lp["lnx_b"], gm_mean, lp["w_br_fox"], lp["w_br_rwkv"],
                lp["w_br_moba"], lp["w_out"])
    return _mlp(xo, lp["norm_mlp"], lp["w_up_mlp"], lp["w_down_mlp"])


def kernel(x_prompt, x_sample, cache_fox_k, cache_fox_v, cache_fox_logf, cache_moba_k, cache_moba_v, state_rwkv, state_rwkv_shift, page_table, norm_mix, w_in, b_forget, qn_fox, kn_fox, qn_moba, kn_moba, rwkv_mu, rwkv_w0, rwkv_w_up, rwkv_a0, rwkv_a_up, rwkv_g_up, rwkv_k_k, rwkv_k_a, rwkv_r_k, rwkv_lnx_w, rwkv_lnx_b, w_br_fox, w_br_rwkv, w_br_moba, w_out, norm_mlp, w_mlp_up, w_mlp_down):
    params = dict(norm_mix=norm_mix, w_in=w_in, b_forget=b_forget, qn_fox=qn_fox, kn_fox=kn_fox,
                  qn_moba=qn_moba, kn_moba=kn_moba, rwkv_mu=rwkv_mu, rwkv_w0=rwkv_w0,
                  rwkv_w_up=rwkv_w_up, rwkv_a0=rwkv_a0, rwkv_a_up=rwkv_a_up, rwkv_g_up=rwkv_g_up,
                  rwkv_k_k=rwkv_k_k, rwkv_k_a=rwkv_k_a, rwkv_r_k=rwkv_r_k, rwkv_lnx_w=rwkv_lnx_w,
                  rwkv_lnx_b=rwkv_lnx_b, w_br_fox=w_br_fox, w_br_rwkv=w_br_rwkv,
                  w_br_moba=w_br_moba, w_out=w_out, norm_mlp=norm_mlp, w_mlp_up=w_mlp_up,
                  w_mlp_down=w_mlp_down)
    depth = w_in.shape[0]
    bp, tp, d = x_prompt.shape
    bs, ts, _ = x_sample.shape
    n_pages = page_table.shape[1]
    page = cache_fox_k.shape[2]
    past_len = n_pages * page
    assert past_len % MOBA_BLOCK == 0 and ts <= MOBA_BLOCK and tp % MOBA_BLOCK == 0
    tile = 512 if tp % 512 == 0 else MOBA_BLOCK
    tm = 256
    assert (bp * tp) % tm == 0 and tp % tm == 0 and (bs * ts) % tm == 0

    hid = jnp.arange(W_HEADS) // HEAD_DIM
    gs_ones = (hid[:, None] == hid[None, :]).astype(BF16)
    rope_p = _rope_tables(jnp.arange(tp, dtype=jnp.int32))
    rope_s = tuple(jnp.tile(a, (bs, 1)) for a in _rope_tables(past_len + jnp.arange(ts, dtype=jnp.int32)))
    n_pool = cache_fox_logf.shape[0]
    lf_rows = cache_fox_logf.reshape(n_pool, depth, 1, page * N_HEADS)

    xp = x_prompt.reshape(bp * tp, d)
    xs = x_sample.reshape(bs * ts, d)
    rows_p = [[] for _ in range(7)]
    rows_s = [[] for _ in range(7)]
    for l in range(depth):
        lp = _layer_params(l, params)

        pc = _mixers_common(xp, lp, gs_ones, gs_ones, rope_p,
                            jnp.zeros((bp, RWKV_COLS), F32),
                            jnp.zeros((bp, N_HEADS, HEAD_DIM, HEAD_DIM), F32), bp, tp, tp // tm)
        b3 = lambda a: a.reshape(bp, tp, W_HEADS)
        ck_t = jnp.swapaxes(pc["c"].reshape(bp, tp, N_HEADS), 1, 2)
        y_fox = _flash("fox", [b3(pc["fq"])], b3(pc["fk"]), b3(pc["fv"]), ck_t, tile)
        qa_e, qa_o = _moba_gate(b3(pc["mq"]), b3(pc["mk"]))
        y_moba = _flash("moba", [qa_e, qa_o], b3(pc["mk"]), b3(pc["mv"]), None, tile)
        xp = _finish_layer(xp, lp, gs_ones, pc, y_fox.reshape(bp * tp, W_HEADS),
                           y_moba.reshape(bp * tp, W_HEADS))
        h5 = lambda a: a.reshape(bp, tp, N_HEADS, HEAD_DIM)
        for acc, val in zip(rows_p, (h5(pc["fk"]), h5(pc["fv"]), pc["logf"].reshape(bp, tp, N_HEADS),
                                     h5(pc["mk"]), h5(pc["mv"]), pc["s_new"], pc["shift_new"])):
            acc.append(val)

        sc = _mixers_common(xs, lp, gs_ones, gs_ones, rope_s, state_rwkv_shift[:, l],
                            state_rwkv[:, l], bs, ts, 1)
        r3 = lambda a: a.reshape(bs, ts * N_HEADS, HEAD_DIM)
        lfn = jnp.pad(sc["logf"].reshape(bs, 1, ts * N_HEADS),
                      ((0, 0), (0, 0), (0, page * N_HEADS - ts * N_HEADS)))
        y_fox_s = _fox_decode(page_table, r3(sc["fq"]), r3(sc["fk"]), r3(sc["fv"]), lfn,
                              cache_fox_k, cache_fox_v, lf_rows, l)
        y_moba_s = _moba_decode(page_table, r3(sc["mq"]), r3(sc["mk"]), r3(sc["mv"]),
                                cache_moba_k, cache_moba_v, l)
        xs = _finish_layer(xs, lp, gs_ones, sc, y_fox_s.reshape(bs * ts, W_HEADS),
                           y_moba_s.reshape(bs * ts, W_HEADS))
        h5s = lambda a: a.reshape(bs, ts, N_HEADS, HEAD_DIM)
        for acc, val in zip(rows_s, (h5s(sc["fk"]), h5s(sc["fv"]), sc["logf"].reshape(bs, ts, N_HEADS),
                                     h5s(sc["mk"]), h5s(sc["mv"]), sc["s_new"], sc["shift_new"])):
            acc.append(val)

    outs_p = [jnp.stack(a, axis=1) for a in rows_p]
    outs_s = [jnp.stack(a, axis=1) for a in rows_s]
    return (xp.reshape(bp, tp, d), xs.reshape(bs, ts, d), *outs_p, *outs_s)
```

```python
import functools
import math

import numpy as np
import jax
import jax.numpy as jnp
from jax import lax
from jax.experimental import pallas as pl
from jax.experimental.pallas import tpu as pltpu

F32 = jnp.float32
BF16 = jnp.bfloat16

HEAD_DIM = 64
N_HEADS = 8
W_HEADS = N_HEADS * HEAD_DIM
N_PAIRS = N_HEADS // 2
RWKV_COLS = 3 * W_HEADS + 64 + 64 + 128
N_BRANCH = 3
MOBA_BLOCK = 256
MOBA_TOPK = 3
ROPE_THETA = 500000.0
ROPE_DIM = HEAD_DIM // 4
NORM_EPS = 1e-6
LNX_EPS = 64e-5
L2_EPS = 1e-12
LOG2E = 1.4426950408889634
Q_SCALE = HEAD_DIM ** -0.5 * LOG2E

LANES = 128
SUBLANES = 8
VMEM_LIMIT = 56 * 1024 * 1024

NEG = -1e30
BIAS_OFF = -(2.0 ** 100)
N_BIAS_LANES = 3
RWKV_CHUNK = 64


def _cparams(sem):
    return pltpu.CompilerParams(dimension_semantics=sem, vmem_limit_bytes=VMEM_LIMIT)


def _dot(a, b):
    return jnp.dot(a, b, preferred_element_type=F32)


def _dot_nt(a, b):
    return lax.dot_general(a, b, (((1,), (1,)), ((), ())), preferred_element_type=F32)


def _split2(x):
    hi = x.astype(BF16)
    lo = (x - hi.astype(F32)).astype(BF16)
    return hi, lo


def _split3(x):
    hi = x.astype(BF16)
    r1 = x - hi.astype(F32)
    mid = r1.astype(BF16)
    lo = (r1 - mid.astype(F32)).astype(BF16)
    return hi, mid, lo


def _dot1(a, b, dot=_dot):
    return dot(a.astype(BF16), b.astype(BF16))


def _dot3(a, b, dot=_dot):
    ah, al = _split2(a)
    bh, bl = _split2(b)
    return dot(ah, bh) + (dot(ah, bl) + dot(al, bh))


def _dot_exact_lhs(a_bf16, b, dot=_dot):
    h, m, l = _split3(b)
    return dot(a_bf16, h) + (dot(a_bf16, m) + dot(a_bf16, l))


def _head_reduce(x, g_bf16):
    hi, lo = _split2(x)
    return _dot(hi, g_bf16) + _dot(lo, g_bf16)


def _softplus(z):
    return jnp.maximum(z, 0.0) + jnp.log(1.0 + jnp.exp(-jnp.abs(z)))


def _sigmoid(z):
    return 1.0 / (1.0 + jnp.exp(-z))


def _rms(x, g):
    ms = jnp.mean(x * x, axis=-1, keepdims=True)
    return x * lax.rsqrt(ms + NORM_EPS) * g


def _iota(shape, dim):
    return lax.broadcasted_iota(jnp.int32, shape, dim)


def _lane_chunks(s):
    return [s[:, j * LANES:(j + 1) * LANES] for j in range(s.shape[1] // LANES)]


def _softmax_update(s, m_prev, l_prev):
    ch = _lane_chunks(s)
    mx = ch[0]
    for c in ch[1:]:
        mx = jnp.maximum(mx, c)
    m_new = jnp.maximum(m_prev, jnp.max(mx, axis=-1, keepdims=True))
    alpha = jnp.exp2(m_prev - m_new)
    ps = [jnp.exp2(c - m_new) for c in ch]
    lsum = ps[0]
    for p_ in ps[1:]:
        lsum = lsum + p_
    pe = ps[0].astype(BF16) if len(ps) == 1 else jnp.concatenate([p_.astype(BF16) for p_ in ps], axis=1)
    return m_new, alpha, alpha * l_prev + lsum, pe


def _proj_plain_kernel(x_ref, g_ref, w_ref, o_ref, *, act):
    h = _rms(x_ref[...], g_ref[...]).astype(BF16)
    u = _dot(h, w_ref[...])
    if act == "sigmoid":
        u = _sigmoid(u)
    o_ref[...] = u


def _proj_plain(x, g, w, act=None, tm=256):
    m, d = x.shape
    n = w.shape[1]
    return pl.pallas_call(
        functools.partial(_proj_plain_kernel, act=act),
        grid=(m // tm,),
        in_specs=[pl.BlockSpec((tm, d), lambda i: (i, 0)),
                  pl.BlockSpec((1, d), lambda i: (0, 0)),
                  pl.BlockSpec((d, n), lambda i: (0, 0))],
        out_specs=pl.BlockSpec((tm, n), lambda i: (i, 0)),
        out_shape=jax.ShapeDtypeStruct((m, n), F32),
        compiler_params=_cparams(("parallel",)),
    )(x, g, w)


def _proj_fox_kernel(x_ref, g_ref, w_ref, wf_ref, bf_ref, qn_ref, kn_ref, gm_ref, place_ref,
                     qs_ref, qe_ref, qo_ref, ke_ref, ko_ref, k_ref, v_ref, vb_ref, lf_ref,
                     carry_scr, *, tiles_per_seq):
    i = pl.program_id(0)
    tm = x_ref.shape[0]
    h = _rms(x_ref[...], g_ref[...]).astype(BF16)
    u = _dot(h, w_ref[...])
    gm = gm_ref[...]
    fq = u[:, 0:W_HEADS]
    fk = u[:, W_HEADS:2 * W_HEADS]
    q_ms = _head_reduce(fq * fq, gm) * (1.0 / HEAD_DIM)
    k_ms = _head_reduce(fk * fk, gm) * (1.0 / HEAD_DIM)
    q = fq * lax.rsqrt(q_ms + NORM_EPS) * qn_ref[...] * Q_SCALE
    k = fk * lax.rsqrt(k_ms + NORM_EPS) * kn_ref[...]
    v = u[:, 2 * W_HEADS:3 * W_HEADS]
    k_ref[...] = k
    v_ref[...] = v
    vb_ref[...] = v.astype(BF16)
    z = _dot(h, wf_ref[...]) + bf_ref[...]
    lane = _iota((1, LANES), 1)
    lf = jnp.where(lane < N_HEADS, -_softplus(-z), 0.0)
    lf_ref[...] = lf[:, 0:N_HEADS]

    @pl.when(i % tiles_per_seq == 0)
    def _():
        carry_scr[...] = jnp.zeros_like(carry_scr)

    tri = (_iota((tm, tm), 0) >= _iota((tm, tm), 1)).astype(BF16)
    c = _dot_exact_lhs(tri, lf) + carry_scr[...]
    carry_scr[...] = c[tm - 1:tm, :]

    lane512 = _iota((1, W_HEADS), 1)
    even = (lane512 // HEAD_DIM) % 2 == 0
    lt = lane512 % LANES
    ones_e = jnp.where(jnp.logical_and(lt >= HEAD_DIM, lt < HEAD_DIM + N_BIAS_LANES), 1.0, 0.0)
    ones_o = jnp.where(lt < N_BIAS_LANES, 1.0, 0.0)
    qs_ref[...] = q.astype(BF16)
    qe_ref[...] = jnp.where(even, q, ones_e).astype(BF16)
    qo_ref[...] = jnp.where(even, ones_o, q).astype(BF16)
    pieces = _split3(c * LOG2E)
    kb_e = _dot(pieces[0], place_ref[0]) + _dot(pieces[1], place_ref[1]) + _dot(pieces[2], place_ref[2])
    kb_o = _dot(pieces[0], place_ref[3]) + _dot(pieces[1], place_ref[4]) + _dot(pieces[2], place_ref[5])
    ke_ref[...] = (jnp.where(even, k, 0.0) + kb_e).astype(BF16)
    ko_ref[...] = (jnp.where(even, 0.0, k) + kb_o).astype(BF16)


def _bias_placement():
    pm = np.zeros((2 * N_BIAS_LANES, LANES, W_HEADS), np.float32)
    for h in range(N_HEADS):
        p = h // 2
        for j in range(N_BIAS_LANES):
            if h % 2 == 0:
                pm[j, h, p * LANES + HEAD_DIM + j] = -1.0
            else:
                pm[N_BIAS_LANES + j, h, p * LANES + j] = -1.0
    return jnp.asarray(pm, BF16)


def _proj_fox(x, g, w, wf, bfg, qn, kn, gm, place, tiles_per_seq, tm=256):
    m, d = x.shape
    row = lambda i: (i, 0)
    fix = lambda i: (0, 0)
    act = pl.BlockSpec((tm, W_HEADS), row)
    bf_out = jax.ShapeDtypeStruct((m, W_HEADS), BF16)
    f_out = jax.ShapeDtypeStruct((m, W_HEADS), F32)
    return pl.pallas_call(
        functools.partial(_proj_fox_kernel, tiles_per_seq=tiles_per_seq),
        grid=(m // tm,),
        in_specs=[pl.BlockSpec((tm, d), row), pl.BlockSpec((1, d), fix),
                  pl.BlockSpec((d, 3 * W_HEADS), fix), pl.BlockSpec((d, LANES), fix),
                  pl.BlockSpec((1, LANES), fix), pl.BlockSpec((1, W_HEADS), fix),
                  pl.BlockSpec((1, W_HEADS), fix), pl.BlockSpec((W_HEADS, W_HEADS), fix),
                  pl.BlockSpec((2 * N_BIAS_LANES, LANES, W_HEADS), lambda i: (0, 0, 0))],
        out_specs=[act] * 8 + [pl.BlockSpec((tm, N_HEADS), row)],
        out_shape=[bf_out] * 5 + [f_out, f_out, bf_out, jax.ShapeDtypeStruct((m, N_HEADS), F32)],
        scratch_shapes=[pltpu.VMEM((1, LANES), F32)],
        compiler_params=_cparams(("arbitrary",)),
    )(x, g, w, wf, bfg, qn, kn, gm, place)


def _rotary(x, cos, sp, sm):
    outs = []
    for j in range(W_HEADS // LANES):
        xs = x[:, j * LANES:(j + 1) * LANES]
        outs.append(xs * cos + pltpu.roll(xs, ROPE_DIM // 2, 1) * sp
                    + pltpu.roll(xs, LANES - ROPE_DIM // 2, 1) * sm)
    return jnp.concatenate(outs, axis=1)


def _proj_moba_kernel(x_ref, g_ref, w_ref, qn_ref, kn_ref, gm_ref, cos_ref, sp_ref, sm_ref,
                      q_ref, k_ref, v_ref, vb_ref):
    h = _rms(x_ref[...], g_ref[...]).astype(BF16)
    u = _dot(h, w_ref[...])
    gm = gm_ref[...]
    mq = u[:, 0:W_HEADS]
    mk = u[:, W_HEADS:2 * W_HEADS]
    q_ms = _head_reduce(mq * mq, gm) * (1.0 / HEAD_DIM)
    k_ms = _head_reduce(mk * mk, gm) * (1.0 / HEAD_DIM)
    cos, sp, sm = cos_ref[...], sp_ref[...], sm_ref[...]
    q_ref[...] = _rotary(mq * lax.rsqrt(q_ms + NORM_EPS) * qn_ref[...], cos, sp, sm)
    k_ref[...] = _rotary(mk * lax.rsqrt(k_ms + NORM_EPS) * kn_ref[...], cos, sp, sm)
    v = u[:, 2 * W_HEADS:3 * W_HEADS]
    v_ref[...] = v
    vb_ref[...] = v.astype(BF16)


def _proj_moba(x, g, w, qn, kn, gm, cos, sp, sm, tm=256):
    m, d = x.shape
    ntab = cos.shape[0] // tm
    row = lambda i: (i, 0)
    fix = lambda i: (0, 0)
    tab = lambda i: (i % ntab, 0)
    f_out = jax.ShapeDtypeStruct((m, W_HEADS), F32)
    return pl.pallas_call(
        _proj_moba_kernel,
        grid=(m // tm,),
        in_specs=[pl.BlockSpec((tm, d), row), pl.BlockSpec((1, d), fix),
                  pl.BlockSpec((d, 3 * W_HEADS), fix), pl.BlockSpec((1, W_HEADS), fix),
                  pl.BlockSpec((1, W_HEADS), fix), pl.BlockSpec((W_HEADS, W_HEADS), fix),
                  pl.BlockSpec((tm, LANES), tab), pl.BlockSpec((tm, LANES), tab),
                  pl.BlockSpec((tm, LANES), tab)],
        out_specs=[pl.BlockSpec((tm, W_HEADS), row)] * 4,
        out_shape=[f_out, f_out, f_out, jax.ShapeDtypeStruct((m, W_HEADS), BF16)],
        compiler_params=_cparams(("parallel",)),
    )(x, g, w, qn, kn, gm, cos, sp, sm)


def _flash_kernel(qt_ref, kt_ref, qe_ref, qo_ref, ke_ref, ko_ref, v_ref, o_ref, m_scr, l_scr, acc_scr,
                  *, tq, tk):
    s_idx = pl.program_id(2)
    qi = qt_ref[s_idx]
    ki = kt_ref[s_idx]

    @pl.when(ki == 0)
    def _():
        m_scr[...] = jnp.full_like(m_scr, NEG)
        l_scr[...] = jnp.zeros_like(l_scr)
        acc_scr[...] = jnp.zeros_like(acc_scr)

    def body(diagonal):
        v = v_ref[0]
        for h, (q_ref, k_ref) in enumerate(((qe_ref, ke_ref), (qo_ref, ko_ref))):
            s = _dot_nt(q_ref[0], k_ref[0])
            if diagonal:
                s = jnp.where(_iota((tq, tk), 1) <= _iota((tq, tk), 0), s, NEG)
            m_new, alpha, l_new, pe = _softmax_update(s, m_scr[h], l_scr[h])
            l_scr[h] = l_new
            acc_scr[h] = alpha * acc_scr[h] + _dot(pe, v)
            m_scr[h] = m_new

    @pl.when(ki < qi)
    def _():
        body(False)

    @pl.when(ki == qi)
    def _():
        body(True)
        lane = _iota((1, LANES), 1)
        l0 = jnp.sum(l_scr[0], axis=-1, keepdims=True)
        l1 = jnp.sum(l_scr[1], axis=-1, keepdims=True)
        o_ref[0] = jnp.where(lane < HEAD_DIM, acc_scr[0] / l0, acc_scr[1] / l1)


def _tri_tables(nq):
    qt = np.concatenate([np.full(i + 1, i, np.int32) for i in range(nq)])
    kt = np.concatenate([np.arange(i + 1, dtype=np.int32) for i in range(nq)])
    return jnp.asarray(qt), jnp.asarray(kt)


def _flash(qe, qo, ke, ko, v, tile):
    b, t, _ = v.shape
    tq = tk = tile
    qt, kt = _tri_tables(t // tq)
    qmap = lambda bb, p, s, qt_, kt_: (bb, qt_[s], p)
    kmap = lambda bb, p, s, qt_, kt_: (bb, kt_[s], p)
    grid_spec = pltpu.PrefetchScalarGridSpec(
        num_scalar_prefetch=2, grid=(b, N_PAIRS, int(qt.shape[0])),
        in_specs=[pl.BlockSpec((1, tq, LANES), qmap)] * 2 + [pl.BlockSpec((1, tk, LANES), kmap)] * 3,
        out_specs=pl.BlockSpec((1, tq, LANES), qmap),
        scratch_shapes=[pltpu.VMEM((2, tq, LANES), F32)] * 3)
    return pl.pallas_call(
        functools.partial(_flash_kernel, tq=tq, tk=tk),
        grid_spec=grid_spec,
        out_shape=jax.ShapeDtypeStruct((b, t, W_HEADS), F32),
        compiler_params=_cparams(("parallel", "parallel", "arbitrary")),
    )(qt, kt, qe, qo, ke, ko, v)


def _top3_select(gm, n_idx):
    sel = jnp.zeros(gm.shape, jnp.bool_)
    big = jnp.int32(1 << 20)
    for _ in range(MOBA_TOPK):
        mx = jnp.max(gm, axis=-1, keepdims=True)
        is_max = jnp.logical_and(gm == mx, mx > 0.5 * NEG)
        idx = jnp.min(jnp.where(is_max, n_idx, big), axis=-1, keepdims=True)
        pick = n_idx == idx
        sel = jnp.logical_or(sel, pick)
        gm = jnp.where(pick, NEG, gm)
    return sel


def _moba_gate_kernel(q_ref, k_ref, qe_ref, qo_ref, ke_ref, ko_ref, kme_scr, kmo_scr, *, n_blk):
    i = pl.program_id(1)
    rows = q_ref.shape[1]

    @pl.when(i == 0)
    def _():
        kme_scr[...] = jnp.zeros_like(kme_scr)
        kmo_scr[...] = jnp.zeros_like(kmo_scr)

    q = q_ref[0]
    k = k_ref[0]
    lane512 = _iota((1, W_HEADS), 1)
    lane = _iota((1, LANES), 1)
    for p in range(N_PAIRS):
        sl = slice(p * LANES, (p + 1) * LANES)
        qpair = q[:, sl] * Q_SCALE
        kpair = k[:, sl]
        for h in range(2):
            head = 2 * p + h
            qm = jnp.where(lane512 // HEAD_DIM == head, q, 0.0)
            km = kme_scr[...] if h == 0 else kmo_scr[...]
            g = _dot3(qm, km, _dot_nt)
            base = HEAD_DIM if h == 0 else 0
            n_idx = lane - base
            in_half = jnp.logical_and(n_idx >= 0, n_idx < HEAD_DIM)
            valid = jnp.logical_and(n_idx >= 0, n_idx < i)
            sel = _top3_select(jnp.where(valid, g, NEG), n_idx)
            keep = jnp.logical_or(jnp.logical_or(sel, n_idx == i), n_idx >= n_blk)
            bias = jnp.where(keep, 0.0, BIAS_OFF)
            qa = jnp.where(in_half, bias, qpair).astype(BF16)
            ka = jnp.where(in_half, jnp.where(n_idx == i, 1.0, 0.0), kpair).astype(BF16)
            if h == 0:
                qe_ref[0, :, sl] = qa
                ke_ref[0, :, sl] = ka
            else:
                qo_ref[0, :, sl] = qa
                ko_ref[0, :, sl] = ka

    kmean = jnp.sum(k, axis=0, keepdims=True) * (1.0 / rows)
    kme_scr[pl.ds(HEAD_DIM + i, 1), :] = kmean
    kmo_scr[pl.ds(i, 1), :] = kmean


def _moba_gate(q, k):
    b, t, _ = q.shape
    n_blk = t // MOBA_BLOCK
    assert n_blk <= HEAD_DIM
    blk = lambda bb, i: (bb, i, 0)
    return pl.pallas_call(
        functools.partial(_moba_gate_kernel, n_blk=n_blk),
        grid=(b, n_blk),
        in_specs=[pl.BlockSpec((1, MOBA_BLOCK, W_HEADS), blk)] * 2,
        out_specs=[pl.BlockSpec((1, MOBA_BLOCK, W_HEADS), blk)] * 4,
        out_shape=[jax.ShapeDtypeStruct((b, t, W_HEADS), BF16)] * 4,
        scratch_shapes=[pltpu.VMEM((LANES, W_HEADS), F32)] * 2,
        compiler_params=_cparams(("parallel", "arbitrary")),
    )(q, k)


def _rwkv_prep_kernel(u_ref, up_ref, mu_ref, w0_ref, wup_ref, a0_ref, aup_ref, gup_ref,
                      kk_ref_w, ka_ref, rk_ref, gs_ref,
                      r_o, ld_o, k_o, v_o, kk_o, b_o, bonus_o, g_o):
    u = u_ref[...]
    um = u + (up_ref[...] - u) * mu_ref[...]
    r = um[:, 0:W_HEADS]
    k = um[:, W_HEADS:2 * W_HEADS]
    v = um[:, 2 * W_HEADS:3 * W_HEADS]
    wa = um[:, 3 * W_HEADS:3 * W_HEADS + LANES]
    gd = um[:, 3 * W_HEADS + LANES:3 * W_HEADS + 2 * LANES]
    w_log = -_softplus(-(w0_ref[...] + _dot(jnp.tanh(wa).astype(BF16), wup_ref[...]))) - 0.5
    ld_o[...] = -jnp.exp(w_log)
    a = _sigmoid(a0_ref[...] + _dot(wa.astype(BF16), aup_ref[...]))
    g_o[...] = _dot(_sigmoid(gd).astype(BF16), gup_ref[...])
    gs = gs_ref[...]
    kx = k * kk_ref_w[...]
    kk = kx * lax.rsqrt(_head_reduce(kx * kx, gs) + L2_EPS)
    k2 = k * (1.0 + (a - 1.0) * ka_ref[...])
    r_o[...] = r
    k_o[...] = k2
    v_o[...] = v
    kk_o[...] = kk
    b_o[...] = kk * a
    bonus_o[...] = _head_reduce(r * k2 * rk_ref[...], gs) * v


def _rwkv_prep(u, up, mu, w0, wup, a0, aup, gup, k_k, k_a, r_k, gs, tm=256):
    m = u.shape[0]
    row = lambda i: (i, 0)
    fix = lambda i: (0, 0)
    vec = pl.BlockSpec((1, W_HEADS), fix)
    lr = pl.BlockSpec((LANES, W_HEADS), fix)
    return pl.pallas_call(
        _rwkv_prep_kernel,
        grid=(m // tm,),
        in_specs=[pl.BlockSpec((tm, RWKV_COLS), row), pl.BlockSpec((tm, RWKV_COLS), row),
                  pl.BlockSpec((1, RWKV_COLS), fix), vec, lr, vec, lr, lr, vec, vec, vec,
                  pl.BlockSpec((W_HEADS, W_HEADS), fix)],
        out_specs=[pl.BlockSpec((tm, W_HEADS), row)] * 8,
        out_shape=[jax.ShapeDtypeStruct((m, W_HEADS), F32)] * 8,
        compiler_params=_cparams(("parallel",)),
    )(u, up, mu, w0, wup, a0, aup, gup, k_k, k_a, r_k, gs)


def _rwkv_chunks(chains, consts, cc, n_double):
    tri, m0, strict, incl, eye = consts

    def stack(x):
        return jnp.concatenate([jnp.where(m0, x, 0.0), jnp.where(m0, 0.0, x)], axis=0)

    cums = [_dot_exact_lhs(tri, ch[1]) for ch in chains]
    pre = []
    for (r, ld, k, v, kk, b, st), cum in zip(chains, cums):
        g_in = jnp.exp(cum)
        g_inv = jnp.exp(-cum)
        g_end = jnp.exp(cum[cc - 1:cc, :] - cum)
        ar = jnp.concatenate([stack(-kk * jnp.exp(cum - ld)), stack(r * g_in)], axis=0)
        pre.append(dict(ar=ar, b2=stack(b * g_inv), k2=stack(k * g_inv), v2=stack(v),
                        bg2t=jnp.transpose(stack(b * g_end)), kg2t=jnp.transpose(stack(k * g_end)),
                        g_col=jnp.transpose(jnp.broadcast_to(g_in[cc - 1:cc, :], (LANES, LANES))), st=st))
    mbs = [_dot3(c["ar"], c["b2"], _dot_nt) for c in pre]
    mks = [_dot1(c["ar"], c["k2"], _dot_nt) for c in pre]
    arss = [_dot1(c["ar"], c["st"]) for c in pre]
    m_abs = [jnp.where(strict, mb[0:2 * cc], 0.0) for mb in mbs]
    t_invs = [eye + m for m in m_abs]
    pws = m_abs
    for _ in range(n_double - 1):
        pws = [_dot3(pw, pw) for pw in pws]
        t_invs = [t + _dot3(pw, t) for pw, t in zip(pws, t_invs)]
    rhs = [ars[0:2 * cc] + _dot1(jnp.where(strict, mk[0:2 * cc], 0.0), c["v2"])
           for ars, mk, c in zip(arss, mks, pre)]
    z2s = [_dot3(t, x) for t, x in zip(t_invs, rhs)]
    outs = []
    for c, mb, mk, ars, z2 in zip(pre, mbs, mks, arss, z2s):
        n_rb = jnp.where(incl, mb[2 * cc:4 * cc], 0.0)
        n_rk = jnp.where(incl, mk[2 * cc:4 * cc], 0.0)
        y2 = ars[2 * cc:4 * cc] + _dot1(n_rb, z2) + _dot1(n_rk, c["v2"])
        st_new = c["st"] * c["g_col"] + _dot1(c["bg2t"], z2) + _dot1(c["kg2t"], c["v2"])
        outs.append((y2[0:cc] + y2[cc:2 * cc], st_new))
    return outs


def _rwkv_scan_kernel(r_ref, ld_ref, k_ref, v_ref, kk_ref, b_ref, s0_ref, y_ref, sT_ref, st_scr,
                      *, chunk, n_double, nb):
    c = pl.program_id(1)
    cc = chunk

    @pl.when(c == 0)
    def _():
        st_scr[...] = s0_ref[...]

    rr = _iota((2 * cc, 2 * cc), 0)
    cl = _iota((2 * cc, 2 * cc), 1)
    same = (rr // cc) == (cl // cc)
    consts = ((_iota((cc, cc), 0) >= _iota((cc, cc), 1)).astype(BF16),
              _iota((cc, LANES), 1) < HEAD_DIM,
              jnp.logical_and(same, (rr % cc) > (cl % cc)),
              jnp.logical_and(same, (rr % cc) >= (cl % cc)),
              jnp.where(rr == cl, 1.0, 0.0))
    chains = [(bi, p, slice(p * LANES, (p + 1) * LANES)) for bi in range(nb) for p in range(N_PAIRS)]
    loaded = [tuple(x[bi, :, sl] for x in (r_ref, ld_ref, k_ref, v_ref, kk_ref, b_ref)) + (st_scr[bi, p],)
              for bi, p, sl in chains]
    results = _rwkv_chunks(loaded, consts, cc, n_double)
    for (bi, p, sl), (y, st_new) in zip(chains, results):
        y_ref[bi, :, sl] = y
        st_scr[bi, p] = st_new

    @pl.when(c == pl.num_programs(1) - 1)
    def _():
        sT_ref[...] = st_scr[...]


def _rwkv_scan(r, ld, k, v, kk, b, s0_bd, nb):
    bsz, t, _ = r.shape
    cc = RWKV_CHUNK
    assert t % cc == 0 and bsz % nb == 0
    blk = lambda i, c: (i, c, 0)
    st = lambda i, c: (i, 0, 0, 0)
    return pl.pallas_call(
        functools.partial(_rwkv_scan_kernel, chunk=cc, n_double=int(math.log2(cc)), nb=nb),
        grid=(bsz // nb, t // cc),
        in_specs=[pl.BlockSpec((nb, cc, W_HEADS), blk)] * 6
                 + [pl.BlockSpec((nb, N_PAIRS, LANES, LANES), st)],
        out_specs=[pl.BlockSpec((nb, cc, W_HEADS), blk), pl.BlockSpec((nb, N_PAIRS, LANES, LANES), st)],
        out_shape=[jax.ShapeDtypeStruct((bsz, t, W_HEADS), F32),
                   jax.ShapeDtypeStruct((bsz, N_PAIRS, LANES, LANES), F32)],
        scratch_shapes=[pltpu.VMEM((nb, N_PAIRS, LANES, LANES), F32)],
        compiler_params=_cparams(("parallel", "arbitrary")),
    )(r, ld, k, v, kk, b, s0_bd)


def _merge_kernel(x_ref, yf_ref, ym_ref, yr_ref, bonus_ref, g_ref, gates_ref,
                  lw_ref, lb_ref, gm_ref, wf_ref, wr_ref, wm_ref, wo_ref, o_ref):
    d = x_ref.shape[1]
    gm = gm_ref[...]
    yr = yr_ref[...]
    mean = _head_reduce(yr, gm) * (1.0 / HEAD_DIM)
    dv = yr - mean
    var = _head_reduce(dv * dv, gm) * (1.0 / HEAD_DIM)
    yn = dv * lax.rsqrt(var + LNX_EPS) * lw_ref[...] + lb_ref[...]
    y_rwkv = (yn + bonus_ref[...]) * g_ref[...]
    gates = gates_ref[...]
    merged = (gates[:, 0:d] * _dot(yf_ref[...].astype(BF16), wf_ref[...])
              + gates[:, d:2 * d] * _dot(y_rwkv.astype(BF16), wr_ref[...])
              + gates[:, 2 * d:3 * d] * _dot(ym_ref[...].astype(BF16), wm_ref[...]))
    o_ref[...] = x_ref[...] + _dot(merged.astype(BF16), wo_ref[...])


def _merge(x, yf, ym, yr, bonus, g, gates, lw, lb, gm, wf, wr, wm, wo, tm=256):
    m, d = x.shape
    row = lambda i: (i, 0)
    fix = lambda i: (0, 0)
    act = pl.BlockSpec((tm, W_HEADS), row)
    vec = pl.BlockSpec((1, W_HEADS), fix)
    wbr = pl.BlockSpec((W_HEADS, d), fix)
    return pl.pallas_call(
        _merge_kernel,
        grid=(m // tm,),
        in_specs=[pl.BlockSpec((tm, d), row), act, act, act, act, act,
                  pl.BlockSpec((tm, N_BRANCH * d), row), vec, vec,
                  pl.BlockSpec((W_HEADS, W_HEADS), fix), wbr, wbr, wbr, pl.BlockSpec((d, d), fix)],
        out_specs=pl.BlockSpec((tm, d), row),
        out_shape=jax.ShapeDtypeStruct((m, d), F32),
        compiler_params=_cparams(("parallel",)),
    )(x, yf, ym, yr, bonus, g, gates, lw, lb, gm, wf, wr, wm, wo)


def _mlp_kernel(x_ref, g_ref, wu_ref, wd_ref, o_ref, h_scr, acc_scr):
    f = pl.program_id(1)

    @pl.when(f == 0)
    def _():
        h_scr[...] = _rms(x_ref[...], g_ref[...]).astype(BF16)
        acc_scr[...] = x_ref[...]

    a = jnp.maximum(_dot(h_scr[...], wu_ref[...]), 0.0)
    acc_scr[...] += _dot((a * a).astype(BF16), wd_ref[...])

    @pl.when(f == pl.num_programs(1) - 1)
    def _():
        o_ref[...] = acc_scr[...]


def _mlp(x, g, wu, wd, tm=256, tf=2048):
    m, d = x.shape
    dff = wu.shape[1]
    return pl.pallas_call(
        _mlp_kernel,
        grid=(m // tm, dff // tf),
        in_specs=[pl.BlockSpec((tm, d), lambda i, f: (i, 0)), pl.BlockSpec((1, d), lambda i, f: (0, 0)),
                  pl.BlockSpec((d, tf), lambda i, f: (0, f)), pl.BlockSpec((tf, d), lambda i, f: (f, 0))],
        out_specs=pl.BlockSpec((tm, d), lambda i, f: (i, 0)),
        out_shape=jax.ShapeDtypeStruct((m, d), F32),
        scratch_shapes=[pltpu.VMEM((tm, d), BF16), pltpu.VMEM((tm, d), F32)],
        compiler_params=_cparams(("parallel", "arbitrary")),
    )(x, g, wu, wd)


def _qbd_rows(q8):
    rows = jnp.concatenate([jnp.broadcast_to(q8[i:i + 1], (N_HEADS, W_HEADS))
                            for i in range(q8.shape[0])], axis=0)
    keep = (_iota(rows.shape, 1) // HEAD_DIM) == (_iota(rows.shape, 0) % N_HEADS)
    return jnp.where(keep, rows, 0.0)


def _rows_to_tokens(o, n_tok):
    keep = (_iota(o.shape, 1) // HEAD_DIM) == (_iota(o.shape, 0) % N_HEADS)
    return jnp.sum(jnp.where(keep, o, 0.0).reshape(n_tok, N_HEADS, W_HEADS), axis=1)


def _page_matrix(refs):
    mats = [r[0, 0].reshape(W_HEADS, r.shape[-1]).astype(BF16) for r in refs]
    return mats[0] if len(mats) == 1 else jnp.concatenate(mats, axis=1)


def _lane_prefix(x):
    lane = _iota(x.shape, 1)
    sh = 1
    while sh < x.shape[1]:
        x = x + jnp.where(lane >= sh, pltpu.roll(x, sh, 1), 0.0)
        sh *= 2
    return x


def _cum_pages(lf, carry):
    n_rows = lf.shape[0]
    loc = _lane_prefix(lf)
    tot = jnp.broadcast_to(loc[:, LANES - 1:LANES], loc.shape)
    run = tot
    rowi = _iota(loc.shape, 0)
    sh = N_HEADS
    while sh < n_rows:
        run = run + jnp.where(rowi >= sh, pltpu.roll(run, sh, 0), 0.0)
        sh *= 2
    carry_t = carry if n_rows == N_HEADS else jnp.concatenate([carry] * (n_rows // N_HEADS), axis=0)
    return loc + (run - tot) + carry_t, run[n_rows - N_HEADS:, :] + carry


def _tile_rows(x, n):
    return jnp.concatenate([x] * n, axis=0)


def _tile_lanes(x, n):
    return jnp.concatenate([x] * n, axis=1)


def _fox_dec_kernel(pt_ref, q_ref, kn_ref, vn_ref, lfn_ref, *refs, n_pg):
    k_refs = refs[0:n_pg]
    v_refs = refs[n_pg:2 * n_pg]
    lf_refs = refs[2 * n_pg:3 * n_pg]
    o_ref, m_scr, l_scr, acc_scr, carry_scr = refs[3 * n_pg:]
    j = pl.program_id(1)
    n_tok = q_ref.shape[1]

    @pl.when(j == 0)
    def _():
        m_scr[...] = jnp.full_like(m_scr, NEG)
        l_scr[...] = jnp.zeros_like(l_scr)
        acc_scr[...] = jnp.zeros_like(acc_scr)
        carry_scr[...] = jnp.zeros_like(carry_scr)

    qbd = _qbd_rows(q_ref[0].astype(F32)).astype(BF16)

    def attend(s, pv):
        m_new, alpha, l_new, pe = _softmax_update(s, m_scr[...], l_scr[...])
        l_scr[...] = l_new
        acc_scr[...] = _tile_lanes(alpha, W_HEADS // LANES) * acc_scr[...] + pv(pe)
        m_scr[...] = m_new

    lf = jnp.concatenate([r[0, 0] for r in lf_refs], axis=0)
    cum, carry = _cum_pages(lf, carry_scr[...])
    carry_scr[...] = carry
    kt = _page_matrix(k_refs)
    vt = _page_matrix(v_refs)
    bias = jnp.concatenate([_tile_rows(cum[i * N_HEADS:(i + 1) * N_HEADS], n_tok)
                            for i in range(n_pg)], axis=1)
    attend(_dot(qbd, kt) - bias * LOG2E, lambda pe: _dot_nt(pe, vt))

    @pl.when(j == pl.num_programs(1) - 1)
    def _():
        cn, _ = _cum_pages(lfn_ref[0], carry_scr[...])
        sn = _dot_nt(qbd, kn_ref[0].astype(BF16)) - _tile_rows(cn, n_tok) * LOG2E
        ok = _iota(sn.shape, 1) <= _iota(sn.shape, 0) // N_HEADS
        vn = vn_ref[0].astype(BF16)
        attend(jnp.where(ok, sn, NEG), lambda pe: _dot(pe, vn))
        l_tot = jnp.sum(l_scr[...], axis=-1, keepdims=True)
        o_ref[0] = _rows_to_tokens(acc_scr[...] / l_tot, n_tok)


def _page_specs(n_pg, tail_shape, layer):
    def one(i):
        return pl.BlockSpec((1, 1) + tail_shape,
                            lambda bb, j, pt, i=i: (pt[bb, j * n_pg + i], layer) + (0,) * len(tail_shape))
    return [one(i) for i in range(n_pg)]


def _fox_decode(page_table, q, kn, vn, lfn, cache_kt, cache_vt, cache_lft, layer, n_pg=8):
    bsz, n_pages = page_table.shape
    page = cache_kt.shape[-1]
    n_tok = q.shape[1]
    rows = n_tok * N_HEADS
    assert n_pages % n_pg == 0 and page == LANES and n_tok <= page
    seq = lambda bb, j, pt: (bb, 0, 0)
    in_specs = [pl.BlockSpec((1, n_tok, W_HEADS), seq), pl.BlockSpec((1, page, W_HEADS), seq),
                pl.BlockSpec((1, page, W_HEADS), seq), pl.BlockSpec((1, N_HEADS, page), seq)]
    in_specs += _page_specs(n_pg, (N_HEADS, HEAD_DIM, page), layer) * 2
    in_specs += _page_specs(n_pg, (N_HEADS, page), layer)
    grid_spec = pltpu.PrefetchScalarGridSpec(
        num_scalar_prefetch=1, grid=(bsz, n_pages // n_pg), in_specs=in_specs,
        out_specs=pl.BlockSpec((1, n_tok, W_HEADS), seq),
        scratch_shapes=[pltpu.VMEM((rows, LANES), F32), pltpu.VMEM((rows, LANES), F32),
                        pltpu.VMEM((rows, W_HEADS), F32), pltpu.VMEM((N_HEADS, LANES), F32)])
    return pl.pallas_call(
        functools.partial(_fox_dec_kernel, n_pg=n_pg),
        grid_spec=grid_spec,
        out_shape=jax.ShapeDtypeStruct((bsz, n_tok, W_HEADS), F32),
        compiler_params=_cparams(("parallel", "arbitrary")),
    )(page_table, q, kn, vn, lfn, *([cache_kt] * n_pg), *([cache_vt] * n_pg), *([cache_lft] * n_pg))


def _moba_dec_kernel(pt_ref, q_ref, kn_ref, vn_ref, *refs, n_pg, n_blk):
    k_refs = refs[0:n_pg]
    v_refs = refs[n_pg:2 * n_pg]
    o_ref, g_scr, m_scr, l_scr, o_scr = refs[2 * n_pg:]
    n = pl.program_id(1)
    n_tok = q_ref.shape[1]
    rows = n_tok * N_HEADS
    lane = _iota((1, LANES), 1)

    @pl.when(n == 0)
    def _():
        g_scr[...] = jnp.zeros_like(g_scr)
        m_scr[...] = jnp.full_like(m_scr, NEG)
        l_scr[...] = jnp.zeros_like(l_scr)

    qh, ql = _split2(_qbd_rows(q_ref[0] * Q_SCALE))
    kt = _page_matrix(k_refs)
    vt = _page_matrix(v_refs)
    s2 = _dot(jnp.concatenate([qh, ql], axis=0), kt)
    s = s2[0:rows]
    gate = jnp.sum(s + s2[rows:2 * rows], axis=-1, keepdims=True)
    onehot = lane == n
    g_scr[...] = jnp.where(onehot, gate, g_scr[...])
    ch = _lane_chunks(s)
    mx = ch[0]
    for c in ch[1:]:
        mx = jnp.maximum(mx, c)
    m_n = jnp.max(mx, axis=-1, keepdims=True)
    ps = [jnp.exp2(c - m_n) for c in ch]
    lsum = ps[0]
    for p_ in ps[1:]:
        lsum = lsum + p_
    m_scr[...] = jnp.where(onehot, m_n, m_scr[...])
    l_scr[...] = jnp.where(onehot, jnp.sum(lsum, axis=-1, keepdims=True), l_scr[...])
    o_scr[n] = _dot_nt(jnp.concatenate([p_.astype(BF16) for p_ in ps], axis=1), vt)

    @pl.when(n == n_blk - 1)
    def _():
        sel = _top3_select(jnp.where(lane < n_blk, g_scr[...], NEG),
                           lane + jnp.zeros((rows, LANES), jnp.int32))
        sn = _dot_nt(qh, kn_ref[0].astype(BF16))
        ok = _iota(sn.shape, 1) <= _iota(sn.shape, 0) // N_HEADS
        sn = jnp.where(ok, sn, NEG)
        m_o = jnp.max(sn, axis=-1, keepdims=True)
        pn = jnp.exp2(sn - m_o)
        l_o = jnp.sum(pn, axis=-1, keepdims=True)
        o_o = _dot(pn.astype(BF16), vn_ref[0].astype(BF16))
        m_all = m_scr[...]
        m_tot = jnp.maximum(m_o, jnp.max(jnp.where(sel, m_all, NEG), axis=-1, keepdims=True))
        w = jnp.where(sel, jnp.exp2(m_all - m_tot), 0.0)
        e_o = jnp.exp2(m_o - m_tot)
        l_tot = l_o * e_o + jnp.sum(w * l_scr[...], axis=-1, keepdims=True)
        out = o_o * e_o
        for b_i in range(n_blk):
            out = out + w[:, b_i:b_i + 1] * o_scr[b_i]
        o_ref[0] = _rows_to_tokens(out / l_tot, n_tok)


def _moba_decode(page_table, q, kn, vn, cache_kt, cache_vt, layer):
    bsz, n_pages = page_table.shape
    page = cache_kt.shape[-1]
    assert MOBA_BLOCK % page == 0 and page == LANES
    n_pg = MOBA_BLOCK // page
    assert n_pages % n_pg == 0
    n_blk = n_pages // n_pg
    assert n_blk <= LANES
    n_tok = q.shape[1]
    rows = n_tok * N_HEADS
    seq = lambda bb, j, pt: (bb, 0, 0)
    in_specs = [pl.BlockSpec((1, n_tok, W_HEADS), seq), pl.BlockSpec((1, page, W_HEADS), seq),
                pl.BlockSpec((1, page, W_HEADS), seq)]
    in_specs += _page_specs(n_pg, (N_HEADS, HEAD_DIM, page), layer) * 2
    grid_spec = pltpu.PrefetchScalarGridSpec(
        num_scalar_prefetch=1, grid=(bsz, n_blk), in_specs=in_specs,
        out_specs=pl.BlockSpec((1, n_tok, W_HEADS), seq),
        scratch_shapes=[pltpu.VMEM((rows, LANES), F32), pltpu.VMEM((rows, LANES), F32),
                        pltpu.VMEM((rows, LANES), F32), pltpu.VMEM((n_blk, rows, W_HEADS), F32)])
    return pl.pallas_call(
        functools.partial(_moba_dec_kernel, n_pg=n_pg, n_blk=n_blk),
        grid_spec=grid_spec,
        out_shape=jax.ShapeDtypeStruct((bsz, n_tok, W_HEADS), F32),
        compiler_params=_cparams(("parallel", "arbitrary")),
    )(page_table, q, kn, vn, *([cache_kt] * n_pg), *([cache_vt] * n_pg))


def _rope_tables(pos):
    half = ROPE_DIM // 2
    inv = ROPE_THETA ** (-jnp.arange(half, dtype=F32) / half)
    ang = pos.astype(F32)[:, None] * inv[None, :]
    cos, sin = jnp.cos(ang), jnp.sin(ang)
    t = pos.shape[0]
    one = jnp.ones((t, HEAD_DIM - ROPE_DIM), F32)
    zero = jnp.zeros((t, HEAD_DIM - ROPE_DIM), F32)
    z8 = jnp.zeros((t, half), F32)
    cos_h = jnp.concatenate([cos, cos, one], axis=1)
    sp_h = jnp.concatenate([z8, sin, zero], axis=1)
    sm_h = jnp.concatenate([-sin, z8, zero], axis=1)
    tile2 = lambda a: jnp.concatenate([a, a], axis=1)
    return tile2(cos_h), tile2(sp_h), tile2(sm_h)


def _layer_params(l, p):
    d = p["w_in"].shape[1]
    w_in = p["w_in"][l]
    o = 0
    w_fox = w_in[:, o:o + 3 * W_HEADS]; o += 3 * W_HEADS
    w_ff = w_in[:, o:o + N_HEADS]; o += N_HEADS
    w_moba = w_in[:, o:o + 3 * W_HEADS]; o += 3 * W_HEADS
    w_ur = w_in[:, o:o + RWKV_COLS]; o += RWKV_COLS
    w_ug = w_in[:, o:o + N_BRANCH * d]
    pad_l = lambda a, n: jnp.pad(a, ((0, 0), (0, n - a.shape[1])))
    tile_h = lambda a: jnp.tile(a, N_HEADS)[None, :]
    zeros64 = jnp.zeros((64, W_HEADS), F32)
    return dict(
        norm_mix=p["norm_mix"][l][None, :],
        w_fox=w_fox.astype(BF16), w_ff=pad_l(w_ff, LANES).astype(BF16),
        b_ff=pad_l(p["b_forget"][l][None, :], LANES),
        w_moba=w_moba.astype(BF16), w_ur=w_ur.astype(BF16), w_ug=w_ug.astype(BF16),
        qn_fox=tile_h(p["qn_fox"][l]), kn_fox=tile_h(p["kn_fox"][l]),
        qn_moba=tile_h(p["qn_moba"][l]), kn_moba=tile_h(p["kn_moba"][l]),
        mu=p["rwkv_mu"][l][None, :], w0=p["rwkv_w0"][l][None, :], a0=p["rwkv_a0"][l][None, :],
        w_up=jnp.concatenate([p["rwkv_w_up"][l], zeros64], axis=0).astype(BF16),
        a_up=jnp.concatenate([zeros64, p["rwkv_a_up"][l]], axis=0).astype(BF16),
        g_up=p["rwkv_g_up"][l].astype(BF16),
        k_k=p["rwkv_k_k"][l][None, :], k_a=p["rwkv_k_a"][l][None, :],
        r_k=p["rwkv_r_k"][l].reshape(1, W_HEADS),
        lnx_w=p["rwkv_lnx_w"][l][None, :], lnx_b=p["rwkv_lnx_b"][l][None, :],
        w_br_fox=p["w_br_fox"][l].astype(BF16), w_br_rwkv=p["w_br_rwkv"][l].astype(BF16),
        w_br_moba=p["w_br_moba"][l].astype(BF16), w_out=p["w_out"][l].astype(BF16),
        norm_mlp=p["norm_mlp"][l][None, :],
        w_up_mlp=p["w_mlp_up"][l].astype(BF16), w_down_mlp=p["w_mlp_down"][l].astype(BF16),
    )


def _state_to_blockdiag(s):
    b = s.shape[0]
    st = jnp.swapaxes(s, -1, -2).reshape(b, N_PAIRS, 2, HEAD_DIM, HEAD_DIM)
    z = jnp.zeros_like(st[:, :, 0])
    top = jnp.concatenate([st[:, :, 0], z], axis=-1)
    bot = jnp.concatenate([z, st[:, :, 1]], axis=-1)
    return jnp.concatenate([top, bot], axis=-2)


def _blockdiag_to_state(sbd):
    b = sbd.shape[0]
    s0 = sbd[:, :, 0:HEAD_DIM, 0:HEAD_DIM]
    s1 = sbd[:, :, HEAD_DIM:, HEAD_DIM:]
    st = jnp.stack([s0, s1], axis=2).reshape(b, N_HEADS, HEAD_DIM, HEAD_DIM)
    return jnp.swapaxes(st, -1, -2)


def _mixers_common(x2, lp, gs_ones, place, rope, shift0, s0, bsz, t, tiles_per_seq):
    m = x2.shape[0]
    qs, qe, qo, ke, ko, fk, fv, fvb, logf = _proj_fox(
        x2, lp["norm_mix"], lp["w_fox"], lp["w_ff"], lp["b_ff"], lp["qn_fox"], lp["kn_fox"],
        gs_ones, place, tiles_per_seq)
    mq, mk, mv, mvb = _proj_moba(x2, lp["norm_mix"], lp["w_moba"], lp["qn_moba"], lp["kn_moba"],
                                 gs_ones, *rope)
    ur = _proj_plain(x2, lp["norm_mix"], lp["w_ur"])
    gates = _proj_plain(x2, lp["norm_mix"], lp["w_ug"], act="sigmoid")
    ur3 = ur.reshape(bsz, t, RWKV_COLS)
    u_prev = jnp.concatenate([shift0[:, None, :], ur3[:, :-1]], axis=1).reshape(m, RWKV_COLS)
    r, ld, k2, v, kk, b, bonus, g = _rwkv_prep(
        ur, u_prev, lp["mu"], lp["w0"], lp["w_up"], lp["a0"], lp["a_up"], lp["g_up"],
        lp["k_k"], lp["k_a"], lp["r_k"], gs_ones)
    t_pad = -(-t // RWKV_CHUNK) * RWKV_CHUNK
    seqs = [a.reshape(bsz, t, W_HEADS) for a in (r, ld, k2, v, kk, b)]
    if t_pad != t:
        seqs = [jnp.pad(a, ((0, 0), (0, t_pad - t), (0, 0))) for a in seqs]
    y_r, s_bd = _rwkv_scan(*seqs, _state_to_blockdiag(s0), nb=2 if bsz % 2 == 0 else 1)
    y_r = y_r[:, :t].reshape(m, W_HEADS)
    return dict(qs=qs, qe=qe, qo=qo, ke=ke, ko=ko, fk=fk, fv=fv, fvb=fvb, logf=logf,
                mq=mq, mk=mk, mv=mv, mvb=mvb, gates=gates,
                y_r=y_r, bonus=bonus, g=g, s_new=_blockdiag_to_state(s_bd), shift_new=ur3[:, -1])


def _finish_layer(x2, lp, gs_ones, pc, y_fox, y_moba):
    xo = _merge(x2, y_fox, y_moba, pc["y_r"], pc["bonus"], pc["g"], pc["gates"],
                lp["lnx_w"], lp["lnx_b"], gs_ones, lp["w_br_fox"], lp["w_br_rwkv"],
                lp["w_br_moba"], lp["w_out"])
    return _mlp(xo, lp["norm_mlp"], lp["w_up_mlp"], lp["w_down_mlp"])


def _flash_tile(t):
    for tile in (1024, 512, MOBA_BLOCK):
        if t % tile == 0:
            return tile
    raise ValueError("prompt length must be a multiple of the MoBA block")


def kernel(x_prompt, x_sample, cache_fox_k, cache_fox_v, cache_fox_logf, cache_moba_k, cache_moba_v, state_rwkv, state_rwkv_shift, page_table, norm_mix, w_in, b_forget, qn_fox, kn_fox, qn_moba, kn_moba, rwkv_mu, rwkv_w0, rwkv_w_up, rwkv_a0, rwkv_a_up, rwkv_g_up, rwkv_k_k, rwkv_k_a, rwkv_r_k, rwkv_lnx_w, rwkv_lnx_b, w_br_fox, w_br_rwkv, w_br_moba, w_out, norm_mlp, w_mlp_up, w_mlp_down):
    params = dict(norm_mix=norm_mix, w_in=w_in, b_forget=b_forget, qn_fox=qn_fox, kn_fox=kn_fox,
                  qn_moba=qn_moba, kn_moba=kn_moba, rwkv_mu=rwkv_mu, rwkv_w0=rwkv_w0,
                  rwkv_w_up=rwkv_w_up, rwkv_a0=rwkv_a0, rwkv_a_up=rwkv_a_up, rwkv_g_up=rwkv_g_up,
                  rwkv_k_k=rwkv_k_k, rwkv_k_a=rwkv_k_a, rwkv_r_k=rwkv_r_k, rwkv_lnx_w=rwkv_lnx_w,
                  rwkv_lnx_b=rwkv_lnx_b, w_br_fox=w_br_fox, w_br_rwkv=w_br_rwkv,
                  w_br_moba=w_br_moba, w_out=w_out, norm_mlp=norm_mlp, w_mlp_up=w_mlp_up,
                  w_mlp_down=w_mlp_down)
    depth = w_in.shape[0]
    bp, tp, d = x_prompt.shape
    bs, ts, _ = x_sample.shape
    n_pages = page_table.shape[1]
    page = cache_fox_k.shape[2]
    past_len = n_pages * page
    assert past_len % MOBA_BLOCK == 0 and ts <= MOBA_BLOCK and tp % MOBA_BLOCK == 0
    tile = _flash_tile(tp)
    tm = 256
    assert (bp * tp) % tm == 0 and tp % tm == 0 and (bs * ts) % tm == 0

    hid = jnp.arange(W_HEADS) // HEAD_DIM
    gs_ones = (hid[:, None] == hid[None, :]).astype(BF16)
    place = _bias_placement()
    rope_p = _rope_tables(jnp.arange(tp, dtype=jnp.int32))
    rope_s = tuple(jnp.tile(a, (bs, 1)) for a in _rope_tables(past_len + jnp.arange(ts, dtype=jnp.int32)))
    tview = lambda c: jnp.transpose(c, (0, 1, 3, 4, 2))
    fox_kt, fox_vt, moba_kt, moba_vt = (tview(c) for c in (cache_fox_k, cache_fox_v, cache_moba_k, cache_moba_v))
    fox_lft = jnp.transpose(cache_fox_logf, (0, 1, 3, 2))

    xp = x_prompt.reshape(bp * tp, d)
    xs = x_sample.reshape(bs * ts, d)
    rows_p = [[] for _ in range(7)]
    rows_s = [[] for _ in range(7)]
    for l in range(depth):
        lp = _layer_params(l, params)

        pc = _mixers_common(xp, lp, gs_ones, place, rope_p,
                            jnp.zeros((bp, RWKV_COLS), F32),
                            jnp.zeros((bp, N_HEADS, HEAD_DIM, HEAD_DIM), F32), bp, tp, tp // tm)
        b3 = lambda a: a.reshape(bp, tp, W_HEADS)
        y_fox = _flash(b3(pc["qe"]), b3(pc["qo"]), b3(pc["ke"]), b3(pc["ko"]), b3(pc["fvb"]), tile)
        mqe, mqo, mke, mko = _moba_gate(b3(pc["mq"]), b3(pc["mk"]))
        y_moba = _flash(mqe, mqo, mke, mko, b3(pc["mvb"]), tile)
        xp = _finish_layer(xp, lp, gs_ones, pc, y_fox.reshape(bp * tp, W_HEADS),
                           y_moba.reshape(bp * tp, W_HEADS))
        h5 = lambda a: a.reshape(bp, tp, N_HEADS, HEAD_DIM)
        for acc, val in zip(rows_p, (h5(pc["fk"]), h5(pc["fv"]), pc["logf"].reshape(bp, tp, N_HEADS),
                                     h5(pc["mk"]), h5(pc["mv"]), pc["s_new"], pc["shift_new"])):
            acc.append(val)

        sc = _mixers_common(xs, lp, gs_ones, place, rope_s, state_rwkv_shift[:, l],
                            state_rwkv[:, l], bs, ts, 1)
        s3 = lambda a: a.reshape(bs, ts, W_HEADS)
        padt = lambda a: jnp.pad(s3(a), ((0, 0), (0, page - ts), (0, 0)))
        lfn = jnp.pad(jnp.swapaxes(sc["logf"].reshape(bs, ts, N_HEADS), 1, 2),
                      ((0, 0), (0, 0), (0, page - ts)))
        y_fox_s = _fox_decode(page_table, s3(sc["qs"]), padt(sc["fk"]), padt(sc["fv"]), lfn,
                              fox_kt, fox_vt, fox_lft, l)
        y_moba_s = _moba_decode(page_table, s3(sc["mq"]), padt(sc["mk"]), padt(sc["mv"]),
                                moba_kt, moba_vt, l)
        xs = _finish_layer(xs, lp, gs_ones, sc, y_fox_s.reshape(bs * ts, W_HEADS),
                           y_moba_s.reshape(bs * ts, W_HEADS))
        h5s = lambda a: a.reshape(bs, ts, N_HEADS, HEAD_DIM)
        for acc, val in zip(rows_s, (h5s(sc["fk"]), h5s(sc["fv"]), sc["logf"].reshape(bs, ts, N_HEADS),
                                     h5s(sc["mk"]), h5s(sc["mv"]), sc["s_new"], sc["shift_new"])):
            acc.append(val)

    outs_p = [jnp.stack(a, axis=1) for a in rows_p]
    outs_s = [jnp.stack(a, axis=1) for a in rows_s]
    return (xp.reshape(bp, tp, d), xs.reshape(bs, ts, d), *outs_p, *outs_s)
```

```python
import functools
import math

import numpy as np
import jax
import jax.numpy as jnp
from jax import lax
from jax.experimental import pallas as pl
from jax.experimental.pallas import tpu as pltpu

F32 = jnp.float32
BF16 = jnp.bfloat16

HEAD_DIM = 64
N_HEADS = 8
W_HEADS = N_HEADS * HEAD_DIM
N_PAIRS = N_HEADS // 2
RWKV_COLS = 3 * W_HEADS + 64 + 64 + 128
N_BRANCH = 3
MOBA_BLOCK = 256
MOBA_TOPK = 3
ROPE_THETA = 500000.0
ROPE_DIM = HEAD_DIM // 4
NORM_EPS = 1e-6
LNX_EPS = 64e-5
L2_EPS = 1e-12
LOG2E = 1.4426950408889634
Q_SCALE = HEAD_DIM ** -0.5 * LOG2E

LANES = 128
SUBLANES = 8
VMEM_LIMIT = 56 * 1024 * 1024

NEG = -1e30
BIAS_OFF = -(2.0 ** 100)
N_BIAS_LANES = 3
RWKV_CHUNK = 64


def _cparams(sem):
    return pltpu.CompilerParams(dimension_semantics=sem, vmem_limit_bytes=VMEM_LIMIT)


def _dot(a, b):
    return jnp.dot(a, b, preferred_element_type=F32)


def _dot_nt(a, b):
    return lax.dot_general(a, b, (((1,), (1,)), ((), ())), preferred_element_type=F32)


def _split2(x):
    hi = x.astype(BF16)
    lo = (x - hi.astype(F32)).astype(BF16)
    return hi, lo


def _split3(x):
    hi = x.astype(BF16)
    r1 = x - hi.astype(F32)
    mid = r1.astype(BF16)
    lo = (r1 - mid.astype(F32)).astype(BF16)
    return hi, mid, lo


def _dot1(a, b, dot=_dot):
    return dot(a.astype(BF16), b.astype(BF16))


def _dot3(a, b, dot=_dot):
    ah, al = _split2(a)
    bh, bl = _split2(b)
    return dot(ah, bh) + (dot(ah, bl) + dot(al, bh))


def _dot_exact_lhs(a_bf16, b, dot=_dot):
    h, m, l = _split3(b)
    return dot(a_bf16, h) + (dot(a_bf16, m) + dot(a_bf16, l))


def _head_reduce(x, g_bf16):
    hi, lo = _split2(x)
    return _dot(hi, g_bf16) + _dot(lo, g_bf16)


def _softplus(z):
    return jnp.maximum(z, 0.0) + jnp.log(1.0 + jnp.exp(-jnp.abs(z)))


def _sigmoid(z):
    return 1.0 / (1.0 + jnp.exp(-z))


def _rms(x, g):
    ms = jnp.mean(x * x, axis=-1, keepdims=True)
    return x * lax.rsqrt(ms + NORM_EPS) * g


def _iota(shape, dim):
    return lax.broadcasted_iota(jnp.int32, shape, dim)


def _lane_chunks(s):
    return [s[:, j * LANES:(j + 1) * LANES] for j in range(s.shape[1] // LANES)]


def _softmax_update(s, m_prev, l_prev):
    ch = _lane_chunks(s)
    mx = ch[0]
    for c in ch[1:]:
        mx = jnp.maximum(mx, c)
    m_new = jnp.maximum(m_prev, jnp.max(mx, axis=-1, keepdims=True))
    alpha = jnp.exp2(m_prev - m_new)
    ps = [jnp.exp2(c - m_new) for c in ch]
    lsum = ps[0]
    for p_ in ps[1:]:
        lsum = lsum + p_
    pe = ps[0].astype(BF16) if len(ps) == 1 else jnp.concatenate([p_.astype(BF16) for p_ in ps], axis=1)
    return m_new, alpha, alpha * l_prev + lsum, pe


def _proj_plain_kernel(x_ref, g_ref, w_ref, o_ref, *, act):
    h = _rms(x_ref[...], g_ref[...]).astype(BF16)
    u = _dot(h, w_ref[...])
    if act == "sigmoid":
        u = _sigmoid(u)
    o_ref[...] = u


def _proj_plain(x, g, w, act=None, tm=256):
    m, d = x.shape
    n = w.shape[1]
    return pl.pallas_call(
        functools.partial(_proj_plain_kernel, act=act),
        grid=(m // tm,),
        in_specs=[pl.BlockSpec((tm, d), lambda i: (i, 0)),
                  pl.BlockSpec((1, d), lambda i: (0, 0)),
                  pl.BlockSpec((d, n), lambda i: (0, 0))],
        out_specs=pl.BlockSpec((tm, n), lambda i: (i, 0)),
        out_shape=jax.ShapeDtypeStruct((m, n), F32),
        compiler_params=_cparams(("parallel",)),
    )(x, g, w)


def _proj_fox_kernel(x_ref, g_ref, w_ref, wf_ref, bf_ref, qn_ref, kn_ref, gm_ref, place_ref, *rest,
                     tiles_per_seq, n_alias):
    qs_ref, qe_ref, qo_ref, ke_ref, ko_ref, k_ref, v_ref, vb_ref, lf_ref, carry_scr = rest[n_alias:]
    i = pl.program_id(0)
    tm = x_ref.shape[0]
    h = _rms(x_ref[...], g_ref[...]).astype(BF16)
    u = _dot(h, w_ref[...])
    gm = gm_ref[...]
    fq = u[:, 0:W_HEADS]
    fk = u[:, W_HEADS:2 * W_HEADS]
    q_ms = _head_reduce(fq * fq, gm) * (1.0 / HEAD_DIM)
    k_ms = _head_reduce(fk * fk, gm) * (1.0 / HEAD_DIM)
    q = fq * lax.rsqrt(q_ms + NORM_EPS) * qn_ref[...] * Q_SCALE
    k = fk * lax.rsqrt(k_ms + NORM_EPS) * kn_ref[...]
    v = u[:, 2 * W_HEADS:3 * W_HEADS]
    k_ref[...] = k
    v_ref[...] = v
    vb_ref[...] = v.astype(BF16)
    z = _dot(h, wf_ref[...]) + bf_ref[...]
    lane = _iota((1, LANES), 1)
    lf = jnp.where(lane < N_HEADS, -_softplus(-z), 0.0)
    lf_ref[...] = lf[:, 0:N_HEADS]

    @pl.when(i % tiles_per_seq == 0)
    def _():
        carry_scr[...] = jnp.zeros_like(carry_scr)

    tri = (_iota((tm, tm), 0) >= _iota((tm, tm), 1)).astype(BF16)
    c = _dot_exact_lhs(tri, lf) + carry_scr[...]
    carry_scr[...] = c[tm - 1:tm, :]

    lane512 = _iota((1, W_HEADS), 1)
    even = (lane512 // HEAD_DIM) % 2 == 0
    lt = lane512 % LANES
    ones_e = jnp.where(jnp.logical_and(lt >= HEAD_DIM, lt < HEAD_DIM + N_BIAS_LANES), 1.0, 0.0)
    ones_o = jnp.where(lt < N_BIAS_LANES, 1.0, 0.0)
    qs_ref[...] = q.astype(BF16)
    qe_ref[...] = jnp.where(even, q, ones_e).astype(BF16)
    qo_ref[...] = jnp.where(even, ones_o, q).astype(BF16)
    pieces = _split3(c * LOG2E)
    kb_e = _dot(pieces[0], place_ref[0]) + _dot(pieces[1], place_ref[1]) + _dot(pieces[2], place_ref[2])
    kb_o = _dot(pieces[0], place_ref[3]) + _dot(pieces[1], place_ref[4]) + _dot(pieces[2], place_ref[5])
    ke_ref[...] = (jnp.where(even, k, 0.0) + kb_e).astype(BF16)
    ko_ref[...] = (jnp.where(even, 0.0, k) + kb_o).astype(BF16)


def _bias_placement():
    pm = np.zeros((2 * N_BIAS_LANES, LANES, W_HEADS), np.float32)
    for h in range(N_HEADS):
        p = h // 2
        for j in range(N_BIAS_LANES):
            if h % 2 == 0:
                pm[j, h, p * LANES + HEAD_DIM + j] = -1.0
            else:
                pm[N_BIAS_LANES + j, h, p * LANES + j] = -1.0
    return jnp.asarray(pm, BF16)


def _stack_rows(stack, tiles_per_seq, tm):
    k_st, v_st, layer, depth = stack
    srow = lambda i: (((i // tiles_per_seq) * depth + layer) * tiles_per_seq + i % tiles_per_seq, 0)
    return (pl.BlockSpec((tm, W_HEADS), srow), jax.ShapeDtypeStruct(k_st.shape, F32),
            [k_st, v_st], [pl.BlockSpec(memory_space=pl.ANY)] * 2)


def _proj_fox(x, g, w, wf, bfg, qn, kn, gm, place, tiles_per_seq, tm=256, stack=None):
    m, d = x.shape
    row = lambda i: (i, 0)
    fix = lambda i: (0, 0)
    act = pl.BlockSpec((tm, W_HEADS), row)
    bf_out = jax.ShapeDtypeStruct((m, W_HEADS), BF16)
    kv_spec, kv_shape, extra, extra_specs, aliases = act, jax.ShapeDtypeStruct((m, W_HEADS), F32), [], [], {}
    if stack is not None:
        kv_spec, kv_shape, extra, extra_specs = _stack_rows(stack, tiles_per_seq, tm)
        aliases = {9: 5, 10: 6}
    return pl.pallas_call(
        functools.partial(_proj_fox_kernel, tiles_per_seq=tiles_per_seq, n_alias=len(extra)),
        grid=(m // tm,),
        in_specs=[pl.BlockSpec((tm, d), row), pl.BlockSpec((1, d), fix),
                  pl.BlockSpec((d, 3 * W_HEADS), fix), pl.BlockSpec((d, LANES), fix),
                  pl.BlockSpec((1, LANES), fix), pl.BlockSpec((1, W_HEADS), fix),
                  pl.BlockSpec((1, W_HEADS), fix), pl.BlockSpec((W_HEADS, W_HEADS), fix),
                  pl.BlockSpec((2 * N_BIAS_LANES, LANES, W_HEADS), lambda i: (0, 0, 0))] + extra_specs,
        out_specs=[act] * 5 + [kv_spec, kv_spec, act, pl.BlockSpec((tm, N_HEADS), row)],
        out_shape=[bf_out] * 5 + [kv_shape, kv_shape, bf_out, jax.ShapeDtypeStruct((m, N_HEADS), F32)],
        scratch_shapes=[pltpu.VMEM((1, LANES), F32)],
        input_output_aliases=aliases,
        compiler_params=_cparams(("arbitrary",)),
    )(x, g, w, wf, bfg, qn, kn, gm, place, *extra)


def _rotary(x, cos, sp, sm):
    outs = []
    for j in range(W_HEADS // LANES):
        xs = x[:, j * LANES:(j + 1) * LANES]
        outs.append(xs * cos + pltpu.roll(xs, ROPE_DIM // 2, 1) * sp
                    + pltpu.roll(xs, LANES - ROPE_DIM // 2, 1) * sm)
    return jnp.concatenate(outs, axis=1)


def _proj_moba_kernel(x_ref, g_ref, w_ref, qn_ref, kn_ref, gm_ref, cos_ref, sp_ref, sm_ref, *rest, n_alias):
    q_ref, k_ref, v_ref, vb_ref = rest[n_alias:]
    h = _rms(x_ref[...], g_ref[...]).astype(BF16)
    u = _dot(h, w_ref[...])
    gm = gm_ref[...]
    mq = u[:, 0:W_HEADS]
    mk = u[:, W_HEADS:2 * W_HEADS]
    q_ms = _head_reduce(mq * mq, gm) * (1.0 / HEAD_DIM)
    k_ms = _head_reduce(mk * mk, gm) * (1.0 / HEAD_DIM)
    cos, sp, sm = cos_ref[...], sp_ref[...], sm_ref[...]
    q_ref[...] = _rotary(mq * lax.rsqrt(q_ms + NORM_EPS) * qn_ref[...], cos, sp, sm)
    k_ref[...] = _rotary(mk * lax.rsqrt(k_ms + NORM_EPS) * kn_ref[...], cos, sp, sm)
    v = u[:, 2 * W_HEADS:3 * W_HEADS]
    v_ref[...] = v
    vb_ref[...] = v.astype(BF16)


def _proj_moba(x, g, w, qn, kn, gm, cos, sp, sm, tiles_per_seq, tm=256, stack=None):
    m, d = x.shape
    ntab = cos.shape[0] // tm
    row = lambda i: (i, 0)
    fix = lambda i: (0, 0)
    tab = lambda i: (i % ntab, 0)
    act = pl.BlockSpec((tm, W_HEADS), row)
    f_out = jax.ShapeDtypeStruct((m, W_HEADS), F32)
    kv_spec, kv_shape, extra, extra_specs, aliases = act, f_out, [], [], {}
    if stack is not None:
        kv_spec, kv_shape, extra, extra_specs = _stack_rows(stack, tiles_per_seq, tm)
        aliases = {9: 1, 10: 2}
    return pl.pallas_call(
        functools.partial(_proj_moba_kernel, n_alias=len(extra)),
        grid=(m // tm,),
        in_specs=[pl.BlockSpec((tm, d), row), pl.BlockSpec((1, d), fix),
                  pl.BlockSpec((d, 3 * W_HEADS), fix), pl.BlockSpec((1, W_HEADS), fix),
                  pl.BlockSpec((1, W_HEADS), fix), pl.BlockSpec((W_HEADS, W_HEADS), fix),
                  pl.BlockSpec((tm, LANES), tab), pl.BlockSpec((tm, LANES), tab),
                  pl.BlockSpec((tm, LANES), tab)] + extra_specs,
        out_specs=[act, kv_spec, kv_spec, act],
        out_shape=[f_out, kv_shape, kv_shape, jax.ShapeDtypeStruct((m, W_HEADS), BF16)],
        input_output_aliases=aliases,
        compiler_params=_cparams(("parallel",)),
    )(x, g, w, qn, kn, gm, cos, sp, sm, *extra)


def _flash_kernel(qt_ref, kt_ref, qe_ref, qo_ref, ke_ref, ko_ref, v_ref, o_ref, m_scr, l_scr, acc_scr,
                  *, tq, tk):
    s_idx = pl.program_id(2)
    qi = qt_ref[s_idx]
    ki = kt_ref[s_idx]

    @pl.when(ki == 0)
    def _():
        m_scr[...] = jnp.full_like(m_scr, NEG)
        l_scr[...] = jnp.zeros_like(l_scr)
        acc_scr[...] = jnp.zeros_like(acc_scr)

    def body(diagonal):
        v = v_ref[0]
        for h, (q_ref, k_ref) in enumerate(((qe_ref, ke_ref), (qo_ref, ko_ref))):
            s = _dot_nt(q_ref[0], k_ref[0])
            if diagonal:
                s = jnp.where(_iota((tq, tk), 1) <= _iota((tq, tk), 0), s, NEG)
            m_new, alpha, l_new, pe = _softmax_update(s, m_scr[h], l_scr[h])
            l_scr[h] = l_new
            acc_scr[h] = alpha * acc_scr[h] + _dot(pe, v)
            m_scr[h] = m_new

    @pl.when(ki < qi)
    def _():
        body(False)

    @pl.when(ki == qi)
    def _():
        body(True)
        lane = _iota((1, LANES), 1)
        l0 = jnp.sum(l_scr[0], axis=-1, keepdims=True)
        l1 = jnp.sum(l_scr[1], axis=-1, keepdims=True)
        o_ref[0] = jnp.where(lane < HEAD_DIM, acc_scr[0] / l0, acc_scr[1] / l1)


def _tri_tables(nq):
    qt = np.concatenate([np.full(i + 1, i, np.int32) for i in range(nq)])
    kt = np.concatenate([np.arange(i + 1, dtype=np.int32) for i in range(nq)])
    return jnp.asarray(qt), jnp.asarray(kt)


def _flash(qe, qo, ke, ko, v, tile):
    b, t, _ = v.shape
    tq = tk = tile
    qt, kt = _tri_tables(t // tq)
    qmap = lambda bb, p, s, qt_, kt_: (bb, qt_[s], p)
    kmap = lambda bb, p, s, qt_, kt_: (bb, kt_[s], p)
    grid_spec = pltpu.PrefetchScalarGridSpec(
        num_scalar_prefetch=2, grid=(b, N_PAIRS, int(qt.shape[0])),
        in_specs=[pl.BlockSpec((1, tq, LANES), qmap)] * 2 + [pl.BlockSpec((1, tk, LANES), kmap)] * 3,
        out_specs=pl.BlockSpec((1, tq, LANES), qmap),
        scratch_shapes=[pltpu.VMEM((2, tq, LANES), F32)] * 3)
    return pl.pallas_call(
        functools.partial(_flash_kernel, tq=tq, tk=tk),
        grid_spec=grid_spec,
        out_shape=jax.ShapeDtypeStruct((b, t, W_HEADS), F32),
        compiler_params=_cparams(("parallel", "parallel", "arbitrary")),
    )(qt, kt, qe, qo, ke, ko, v)


def _top3_select(gm, n_idx):
    sel = jnp.zeros(gm.shape, jnp.bool_)
    big = jnp.int32(1 << 20)
    for _ in range(MOBA_TOPK):
        mx = jnp.max(gm, axis=-1, keepdims=True)
        is_max = jnp.logical_and(gm == mx, mx > 0.5 * NEG)
        idx = jnp.min(jnp.where(is_max, n_idx, big), axis=-1, keepdims=True)
        pick = n_idx == idx
        sel = jnp.logical_or(sel, pick)
        gm = jnp.where(pick, NEG, gm)
    return sel


def _moba_gate_kernel(q_ref, k_ref, qe_ref, qo_ref, ke_ref, ko_ref, kme_scr, kmo_scr, *, n_blk):
    i = pl.program_id(1)
    rows = q_ref.shape[1]

    @pl.when(i == 0)
    def _():
        kme_scr[...] = jnp.zeros_like(kme_scr)
        kmo_scr[...] = jnp.zeros_like(kmo_scr)

    q = q_ref[0]
    k = k_ref[0]
    lane512 = _iota((1, W_HEADS), 1)
    lane = _iota((1, LANES), 1)
    for p in range(N_PAIRS):
        sl = slice(p * LANES, (p + 1) * LANES)
        qpair = q[:, sl] * Q_SCALE
        kpair = k[:, sl]
        for h in range(2):
            head = 2 * p + h
            qm = jnp.where(lane512 // HEAD_DIM == head, q, 0.0)
            km = kme_scr[...] if h == 0 else kmo_scr[...]
            g = _dot3(qm, km, _dot_nt)
            base = HEAD_DIM if h == 0 else 0
            n_idx = lane - base
            in_half = jnp.logical_and(n_idx >= 0, n_idx < HEAD_DIM)
            valid = jnp.logical_and(n_idx >= 0, n_idx < i)
            sel = _top3_select(jnp.where(valid, g, NEG), n_idx)
            keep = jnp.logical_or(jnp.logical_or(sel, n_idx == i), n_idx >= n_blk)
            bias = jnp.where(keep, 0.0, BIAS_OFF)
            qa = jnp.where(in_half, bias, qpair).astype(BF16)
            ka = jnp.where(in_half, jnp.where(n_idx == i, 1.0, 0.0), kpair).astype(BF16)
            if h == 0:
                qe_ref[0, :, sl] = qa
                ke_ref[0, :, sl] = ka
            else:
                qo_ref[0, :, sl] = qa
                ko_ref[0, :, sl] = ka

    kmean = jnp.sum(k, axis=0, keepdims=True) * (1.0 / rows)
    kme_scr[pl.ds(HEAD_DIM + i, 1), :] = kmean
    kmo_scr[pl.ds(i, 1), :] = kmean


def _moba_gate(q, k, k_blk_off=0):
    b, t, _ = q.shape
    n_blk = t // MOBA_BLOCK
    assert n_blk <= HEAD_DIM
    blk = lambda bb, i: (bb, i, 0)
    return pl.pallas_call(
        functools.partial(_moba_gate_kernel, n_blk=n_blk),
        grid=(b, n_blk),
        in_specs=[pl.BlockSpec((1, MOBA_BLOCK, W_HEADS), blk),
                  pl.BlockSpec((1, MOBA_BLOCK, W_HEADS), lambda bb, i: (bb, k_blk_off + i, 0))],
        out_specs=[pl.BlockSpec((1, MOBA_BLOCK, W_HEADS), blk)] * 4,
        out_shape=[jax.ShapeDtypeStruct((b, t, W_HEADS), BF16)] * 4,
        scratch_shapes=[pltpu.VMEM((LANES, W_HEADS), F32)] * 2,
        compiler_params=_cparams(("parallel", "arbitrary")),
    )(q, k)


def _rwkv_prep_kernel(u_ref, up_ref, mu_ref, w0_ref, wup_ref, a0_ref, aup_ref, gup_ref,
                      kk_ref_w, ka_ref, rk_ref, gs_ref,
                      r_o, ld_o, k_o, v_o, kk_o, b_o, bonus_o, g_o):
    u = u_ref[...]
    um = u + (up_ref[...] - u) * mu_ref[...]
    r = um[:, 0:W_HEADS]
    k = um[:, W_HEADS:2 * W_HEADS]
    v = um[:, 2 * W_HEADS:3 * W_HEADS]
    wa = um[:, 3 * W_HEADS:3 * W_HEADS + LANES]
    gd = um[:, 3 * W_HEADS + LANES:3 * W_HEADS + 2 * LANES]
    w_log = -_softplus(-(w0_ref[...] + _dot(jnp.tanh(wa).astype(BF16), wup_ref[...]))) - 0.5
    ld_o[...] = -jnp.exp(w_log)
    a = _sigmoid(a0_ref[...] + _dot(wa.astype(BF16), aup_ref[...]))
    g_o[...] = _dot(_sigmoid(gd).astype(BF16), gup_ref[...])
    gs = gs_ref[...]
    kx = k * kk_ref_w[...]
    kk = kx * lax.rsqrt(_head_reduce(kx * kx, gs) + L2_EPS)
    k2 = k * (1.0 + (a - 1.0) * ka_ref[...])
    r_o[...] = r
    k_o[...] = k2
    v_o[...] = v
    kk_o[...] = kk
    b_o[...] = kk * a
    bonus_o[...] = _head_reduce(r * k2 * rk_ref[...], gs) * v


def _rwkv_prep(u, up, mu, w0, wup, a0, aup, gup, k_k, k_a, r_k, gs, tm=256):
    m = u.shape[0]
    row = lambda i: (i, 0)
    fix = lambda i: (0, 0)
    vec = pl.BlockSpec((1, W_HEADS), fix)
    lr = pl.BlockSpec((LANES, W_HEADS), fix)
    return pl.pallas_call(
        _rwkv_prep_kernel,
        grid=(m // tm,),
        in_specs=[pl.BlockSpec((tm, RWKV_COLS), row), pl.BlockSpec((tm, RWKV_COLS), row),
                  pl.BlockSpec((1, RWKV_COLS), fix), vec, lr, vec, lr, lr, vec, vec, vec,
                  pl.BlockSpec((W_HEADS, W_HEADS), fix)],
        out_specs=[pl.BlockSpec((tm, W_HEADS), row)] * 8,
        out_shape=[jax.ShapeDtypeStruct((m, W_HEADS), F32)] * 8,
        compiler_params=_cparams(("parallel",)),
    )(u, up, mu, w0, wup, a0, aup, gup, k_k, k_a, r_k, gs)


def _rwkv_chunks(chains, consts, cc, n_double):
    tri, m0, strict, incl, eye = consts

    def stack(x):
        return jnp.concatenate([jnp.where(m0, x, 0.0), jnp.where(m0, 0.0, x)], axis=0)

    cums = [_dot_exact_lhs(tri, ch[1]) for ch in chains]
    pre = []
    for (r, ld, k, v, kk, b, st), cum in zip(chains, cums):
        g_in = jnp.exp(cum)
        g_inv = jnp.exp(-cum)
        g_end = jnp.exp(cum[cc - 1:cc, :] - cum)
        ar = jnp.concatenate([stack(-kk * jnp.exp(cum - ld)), stack(r * g_in)], axis=0)
        pre.append(dict(ar=ar, b2=stack(b * g_inv), k2=stack(k * g_inv), v2=stack(v),
                        bg2t=jnp.transpose(stack(b * g_end)), kg2t=jnp.transpose(stack(k * g_end)),
                        g_col=jnp.transpose(jnp.broadcast_to(g_in[cc - 1:cc, :], (LANES, LANES))), st=st))
    mbs = [_dot3(c["ar"], c["b2"], _dot_nt) for c in pre]
    mks = [_dot1(c["ar"], c["k2"], _dot_nt) for c in pre]
    arss = [_dot1(c["ar"], c["st"]) for c in pre]
    m_abs = [jnp.where(strict, mb[0:2 * cc], 0.0) for mb in mbs]
    t_invs = [eye + m for m in m_abs]
    pws = m_abs
    for _ in range(n_double - 1):
        pws = [_dot3(pw, pw) for pw in pws]
        t_invs = [t + _dot3(pw, t) for pw, t in zip(pws, t_invs)]
    rhs = [ars[0:2 * cc] + _dot1(jnp.where(strict, mk[0:2 * cc], 0.0), c["v2"])
           for ars, mk, c in zip(arss, mks, pre)]
    z2s = [_dot3(t, x) for t, x in zip(t_invs, rhs)]
    outs = []
    for c, mb, mk, ars, z2 in zip(pre, mbs, mks, arss, z2s):
        n_rb = jnp.where(incl, mb[2 * cc:4 * cc], 0.0)
        n_rk = jnp.where(incl, mk[2 * cc:4 * cc], 0.0)
        y2 = ars[2 * cc:4 * cc] + _dot1(n_rb, z2) + _dot1(n_rk, c["v2"])
        st_new = c["st"] * c["g_col"] + _dot1(c["bg2t"], z2) + _dot1(c["kg2t"], c["v2"])
        outs.append((y2[0:cc] + y2[cc:2 * cc], st_new))
    return outs


def _rwkv_scan_kernel(r_ref, ld_ref, k_ref, v_ref, kk_ref, b_ref, s0_ref, y_ref, sT_ref, st_scr,
                      *, chunk, n_double, nb):
    c = pl.program_id(1)
    cc = chunk

    @pl.when(c == 0)
    def _():
        st_scr[...] = s0_ref[...]

    rr = _iota((2 * cc, 2 * cc), 0)
    cl = _iota((2 * cc, 2 * cc), 1)
    same = (rr // cc) == (cl // cc)
    consts = ((_iota((cc, cc), 0) >= _iota((cc, cc), 1)).astype(BF16),
              _iota((cc, LANES), 1) < HEAD_DIM,
              jnp.logical_and(same, (rr % cc) > (cl % cc)),
              jnp.logical_and(same, (rr % cc) >= (cl % cc)),
              jnp.where(rr == cl, 1.0, 0.0))
    chains = [(bi, p, slice(p * LANES, (p + 1) * LANES)) for bi in range(nb) for p in range(N_PAIRS)]
    loaded = [tuple(x[bi, :, sl] for x in (r_ref, ld_ref, k_ref, v_ref, kk_ref, b_ref)) + (st_scr[bi, p],)
              for bi, p, sl in chains]
    results = _rwkv_chunks(loaded, consts, cc, n_double)
    for (bi, p, sl), (y, st_new) in zip(chains, results):
        y_ref[bi, :, sl] = y
        st_scr[bi, p] = st_new

    @pl.when(c == pl.num_programs(1) - 1)
    def _():
        sT_ref[...] = st_scr[...]


def _rwkv_scan(r, ld, k, v, kk, b, s0_bd, nb):
    bsz, t, _ = r.shape
    cc = RWKV_CHUNK
    assert t % cc == 0 and bsz % nb == 0
    blk = lambda i, c: (i, c, 0)
    st = lambda i, c: (i, 0, 0, 0)
    return pl.pallas_call(
        functools.partial(_rwkv_scan_kernel, chunk=cc, n_double=int(math.log2(cc)), nb=nb),
        grid=(bsz // nb, t // cc),
        in_specs=[pl.BlockSpec((nb, cc, W_HEADS), blk)] * 6
                 + [pl.BlockSpec((nb, N_PAIRS, LANES, LANES), st)],
        out_specs=[pl.BlockSpec((nb, cc, W_HEADS), blk), pl.BlockSpec((nb, N_PAIRS, LANES, LANES), st)],
        out_shape=[jax.ShapeDtypeStruct((bsz, t, W_HEADS), F32),
                   jax.ShapeDtypeStruct((bsz, N_PAIRS, LANES, LANES), F32)],
        scratch_shapes=[pltpu.VMEM((nb, N_PAIRS, LANES, LANES), F32)],
        compiler_params=_cparams(("parallel", "arbitrary")),
    )(r, ld, k, v, kk, b, s0_bd)


def _merge_kernel(x_ref, yf_ref, ym_ref, yr_ref, bonus_ref, g_ref, gates_ref,
                  lw_ref, lb_ref, gm_ref, wf_ref, wr_ref, wm_ref, wo_ref, o_ref):
    d = x_ref.shape[1]
    gm = gm_ref[...]
    yr = yr_ref[...]
    mean = _head_reduce(yr, gm) * (1.0 / HEAD_DIM)
    dv = yr - mean
    var = _head_reduce(dv * dv, gm) * (1.0 / HEAD_DIM)
    yn = dv * lax.rsqrt(var + LNX_EPS) * lw_ref[...] + lb_ref[...]
    y_rwkv = (yn + bonus_ref[...]) * g_ref[...]
    gates = gates_ref[...]
    merged = (gates[:, 0:d] * _dot(yf_ref[...].astype(BF16), wf_ref[...])
              + gates[:, d:2 * d] * _dot(y_rwkv.astype(BF16), wr_ref[...])
              + gates[:, 2 * d:3 * d] * _dot(ym_ref[...].astype(BF16), wm_ref[...]))
    o_ref[...] = x_ref[...] + _dot(merged.astype(BF16), wo_ref[...])


def _merge(x, yf, ym, yr, bonus, g, gates, lw, lb, gm, wf, wr, wm, wo, tm=256):
    m, d = x.shape
    row = lambda i: (i, 0)
    fix = lambda i: (0, 0)
    act = pl.BlockSpec((tm, W_HEADS), row)
    vec = pl.BlockSpec((1, W_HEADS), fix)
    wbr = pl.BlockSpec((W_HEADS, d), fix)
    return pl.pallas_call(
        _merge_kernel,
        grid=(m // tm,),
        in_specs=[pl.BlockSpec((tm, d), row), act, act, act, act, act,
                  pl.BlockSpec((tm, N_BRANCH * d), row), vec, vec,
                  pl.BlockSpec((W_HEADS, W_HEADS), fix), wbr, wbr, wbr, pl.BlockSpec((d, d), fix)],
        out_specs=pl.BlockSpec((tm, d), row),
        out_shape=jax.ShapeDtypeStruct((m, d), F32),
        compiler_params=_cparams(("parallel",)),
    )(x, yf, ym, yr, bonus, g, gates, lw, lb, gm, wf, wr, wm, wo)


def _mlp_kernel(x_ref, g_ref, wu_ref, wd_ref, o_ref, h_scr, acc_scr):
    f = pl.program_id(1)

    @pl.when(f == 0)
    def _():
        h_scr[...] = _rms(x_ref[...], g_ref[...]).astype(BF16)
        acc_scr[...] = x_ref[...]

    a = jnp.maximum(_dot(h_scr[...], wu_ref[...]), 0.0)
    acc_scr[...] += _dot((a * a).astype(BF16), wd_ref[...])

    @pl.when(f == pl.num_programs(1) - 1)
    def _():
        o_ref[...] = acc_scr[...]


def _mlp(x, g, wu, wd, tf=2048):
    m, d = x.shape
    tm = 512 if m % 512 == 0 else 256
    dff = wu.shape[1]
    return pl.pallas_call(
        _mlp_kernel,
        grid=(m // tm, dff // tf),
        in_specs=[pl.BlockSpec((tm, d), lambda i, f: (i, 0)), pl.BlockSpec((1, d), lambda i, f: (0, 0)),
                  pl.BlockSpec((d, tf), lambda i, f: (0, f)), pl.BlockSpec((tf, d), lambda i, f: (f, 0))],
        out_specs=pl.BlockSpec((tm, d), lambda i, f: (i, 0)),
        out_shape=jax.ShapeDtypeStruct((m, d), F32),
        scratch_shapes=[pltpu.VMEM((tm, d), BF16), pltpu.VMEM((tm, d), F32)],
        compiler_params=_cparams(("parallel", "arbitrary")),
    )(x, g, wu, wd)


def _qbd_rows(q8):
    rows = jnp.concatenate([jnp.broadcast_to(q8[i:i + 1], (N_HEADS, W_HEADS))
                            for i in range(q8.shape[0])], axis=0)
    keep = (_iota(rows.shape, 1) // HEAD_DIM) == (_iota(rows.shape, 0) % N_HEADS)
    return jnp.where(keep, rows, 0.0)


def _rows_to_tokens(o, n_tok):
    keep = (_iota(o.shape, 1) // HEAD_DIM) == (_iota(o.shape, 0) % N_HEADS)
    return jnp.sum(jnp.where(keep, o, 0.0).reshape(n_tok, N_HEADS, W_HEADS), axis=1)


def _page_matrix(refs):
    mats = [r[0, 0].reshape(W_HEADS, r.shape[-1]).astype(BF16) for r in refs]
    return mats[0] if len(mats) == 1 else jnp.concatenate(mats, axis=1)


def _lane_prefix(x):
    lane = _iota(x.shape, 1)
    sh = 1
    while sh < x.shape[1]:
        x = x + jnp.where(lane >= sh, pltpu.roll(x, sh, 1), 0.0)
        sh *= 2
    return x


def _cum_pages(lf, carry):
    n_rows = lf.shape[0]
    loc = _lane_prefix(lf)
    tot = jnp.broadcast_to(loc[:, LANES - 1:LANES], loc.shape)
    run = tot
    rowi = _iota(loc.shape, 0)
    sh = N_HEADS
    while sh < n_rows:
        run = run + jnp.where(rowi >= sh, pltpu.roll(run, sh, 0), 0.0)
        sh *= 2
    carry_t = carry if n_rows == N_HEADS else jnp.concatenate([carry] * (n_rows // N_HEADS), axis=0)
    return loc + (run - tot) + carry_t, run[n_rows - N_HEADS:, :] + carry


def _tile_rows(x, n):
    return jnp.concatenate([x] * n, axis=0)


def _tile_lanes(x, n):
    return jnp.concatenate([x] * n, axis=1)


def _fox_dec_kernel(pt_ref, q_ref, kn_ref, vn_ref, lfn_ref, *refs, n_pg):
    k_refs = refs[0:n_pg]
    v_refs = refs[n_pg:2 * n_pg]
    lf_refs = refs[2 * n_pg:3 * n_pg]
    o_ref, m_scr, l_scr, acc_scr, carry_scr = refs[3 * n_pg:]
    j = pl.program_id(1)
    n_tok = q_ref.shape[1]

    @pl.when(j == 0)
    def _():
        m_scr[...] = jnp.full_like(m_scr, NEG)
        l_scr[...] = jnp.zeros_like(l_scr)
        acc_scr[...] = jnp.zeros_like(acc_scr)
        carry_scr[...] = jnp.zeros_like(carry_scr)

    qbd = _qbd_rows(q_ref[0].astype(F32)).astype(BF16)

    def attend(s, pv):
        m_new, alpha, l_new, pe = _softmax_update(s, m_scr[...], l_scr[...])
        l_scr[...] = l_new
        acc_scr[...] = _tile_lanes(alpha, W_HEADS // LANES) * acc_scr[...] + pv(pe)
        m_scr[...] = m_new

    lf = jnp.concatenate([r[0, 0] for r in lf_refs], axis=0)
    cum, carry = _cum_pages(lf, carry_scr[...])
    carry_scr[...] = carry
    kt = _page_matrix(k_refs)
    vt = _page_matrix(v_refs)
    bias = jnp.concatenate([_tile_rows(cum[i * N_HEADS:(i + 1) * N_HEADS], n_tok)
                            for i in range(n_pg)], axis=1)
    attend(_dot(qbd, kt) - bias * LOG2E, lambda pe: _dot_nt(pe, vt))

    @pl.when(j == pl.num_programs(1) - 1)
    def _():
        cn, _ = _cum_pages(lfn_ref[0], carry_scr[...])
        sn = _dot_nt(qbd, kn_ref[0].astype(BF16)) - _tile_rows(cn, n_tok) * LOG2E
        ok = _iota(sn.shape, 1) <= _iota(sn.shape, 0) // N_HEADS
        vn = vn_ref[0].astype(BF16)
        attend(jnp.where(ok, sn, NEG), lambda pe: _dot(pe, vn))
        l_tot = jnp.sum(l_scr[...], axis=-1, keepdims=True)
        o_ref[0] = _rows_to_tokens(acc_scr[...] / l_tot, n_tok)


def _page_specs(n_pg, tail_shape, layer):
    def one(i):
        return pl.BlockSpec((1, 1) + tail_shape,
                            lambda bb, j, pt, i=i: (pt[bb, j * n_pg + i], layer) + (0,) * len(tail_shape))
    return [one(i) for i in range(n_pg)]


def _fox_decode(page_table, q, kn, vn, lfn, cache_kt, cache_vt, cache_lft, layer):
    bsz, n_pages = page_table.shape
    page = cache_kt.shape[-1]
    n_tok = q.shape[1]
    rows = n_tok * N_HEADS
    n_pg = next(n for n in (16, 8, 4, 2, 1) if n_pages % n == 0)
    assert page == LANES and n_tok <= page
    seq = lambda bb, j, pt: (bb, 0, 0)
    in_specs = [pl.BlockSpec((1, n_tok, W_HEADS), seq), pl.BlockSpec((1, page, W_HEADS), seq),
                pl.BlockSpec((1, page, W_HEADS), seq), pl.BlockSpec((1, N_HEADS, page), seq)]
    in_specs += _page_specs(n_pg, (N_HEADS, HEAD_DIM, page), layer) * 2
    in_specs += _page_specs(n_pg, (N_HEADS, page), layer)
    grid_spec = pltpu.PrefetchScalarGridSpec(
        num_scalar_prefetch=1, grid=(bsz, n_pages // n_pg), in_specs=in_specs,
        out_specs=pl.BlockSpec((1, n_tok, W_HEADS), seq),
        scratch_shapes=[pltpu.VMEM((rows, LANES), F32), pltpu.VMEM((rows, LANES), F32),
                        pltpu.VMEM((rows, W_HEADS), F32), pltpu.VMEM((N_HEADS, LANES), F32)])
    return pl.pallas_call(
        functools.partial(_fox_dec_kernel, n_pg=n_pg),
        grid_spec=grid_spec,
        out_shape=jax.ShapeDtypeStruct((bsz, n_tok, W_HEADS), F32),
        compiler_params=_cparams(("parallel", "arbitrary")),
    )(page_table, q, kn, vn, lfn, *([cache_kt] * n_pg), *([cache_vt] * n_pg), *([cache_lft] * n_pg))


def _moba_dec_kernel(pt_ref, q_ref, kn_ref, vn_ref, *refs, n_pg, n_blk, bps):
    k_refs = refs[0:n_pg]
    v_refs = refs[n_pg:2 * n_pg]
    o_ref, g_scr, m_scr, l_scr, o_scr = refs[2 * n_pg:]
    n = pl.program_id(1)
    n_tok = q_ref.shape[1]
    rows = n_tok * N_HEADS
    lane = _iota((1, LANES), 1)

    @pl.when(n == 0)
    def _():
        g_scr[...] = jnp.zeros_like(g_scr)
        m_scr[...] = jnp.full_like(m_scr, NEG)
        l_scr[...] = jnp.zeros_like(l_scr)

    qh, ql = _split2(_qbd_rows(q_ref[0] * Q_SCALE))
    kt = _page_matrix(k_refs)
    vt = _page_matrix(v_refs)
    s2 = _dot(jnp.concatenate([qh, ql], axis=0), kt)
    bw = kt.shape[1] // bps
    g_new, m_new, l_new = g_scr[...], m_scr[...], l_scr[...]
    for bb in range(bps):
        blk = n * bps + bb
        s = s2[0:rows, bb * bw:(bb + 1) * bw]
        gate = jnp.sum(s + s2[rows:2 * rows, bb * bw:(bb + 1) * bw], axis=-1, keepdims=True)
        ch = _lane_chunks(s)
        mx = ch[0]
        for c in ch[1:]:
            mx = jnp.maximum(mx, c)
        m_n = jnp.max(mx, axis=-1, keepdims=True)
        ps = [jnp.exp2(c - m_n) for c in ch]
        lsum = ps[0]
        for p_ in ps[1:]:
            lsum = lsum + p_
        onehot = lane == blk
        g_new = jnp.where(onehot, gate, g_new)
        m_new = jnp.where(onehot, m_n, m_new)
        l_new = jnp.where(onehot, jnp.sum(lsum, axis=-1, keepdims=True), l_new)
        o_scr[blk] = _dot_nt(jnp.concatenate([p_.astype(BF16) for p_ in ps], axis=1),
                             vt[:, bb * bw:(bb + 1) * bw])
    g_scr[...] = g_new
    m_scr[...] = m_new
    l_scr[...] = l_new

    @pl.when(n == pl.num_programs(1) - 1)
    def _():
        sel = _top3_select(jnp.where(lane < n_blk, g_scr[...], NEG),
                           lane + jnp.zeros((rows, LANES), jnp.int32))
        sn = _dot_nt(qh, kn_ref[0].astype(BF16))
        ok = _iota(sn.shape, 1) <= _iota(sn.shape, 0) // N_HEADS
        sn = jnp.where(ok, sn, NEG)
        m_o = jnp.max(sn, axis=-1, keepdims=True)
        pn = jnp.exp2(sn - m_o)
        l_o = jnp.sum(pn, axis=-1, keepdims=True)
        o_o = _dot(pn.astype(BF16), vn_ref[0].astype(BF16))
        m_all = m_scr[...]
        m_tot = jnp.maximum(m_o, jnp.max(jnp.where(sel, m_all, NEG), axis=-1, keepdims=True))
        w = jnp.where(sel, jnp.exp2(m_all - m_tot), 0.0)
        e_o = jnp.exp2(m_o - m_tot)
        l_tot = l_o * e_o + jnp.sum(w * l_scr[...], axis=-1, keepdims=True)
        out = o_o * e_o
        for b_i in range(n_blk):
            out = out + w[:, b_i:b_i + 1] * o_scr[b_i]
        o_ref[0] = _rows_to_tokens(out / l_tot, n_tok)


def _moba_decode(page_table, q, kn, vn, cache_kt, cache_vt, layer):
    bsz, n_pages = page_table.shape
    page = cache_kt.shape[-1]
    assert MOBA_BLOCK % page == 0 and page == LANES
    pages_per_blk = MOBA_BLOCK // page
    assert n_pages % pages_per_blk == 0
    n_blk = n_pages // pages_per_blk
    assert n_blk <= LANES
    bps = 4 if n_blk % 4 == 0 else 1
    n_pg = bps * pages_per_blk
    n_tok = q.shape[1]
    rows = n_tok * N_HEADS
    seq = lambda bb, j, pt: (bb, 0, 0)
    in_specs = [pl.BlockSpec((1, n_tok, W_HEADS), seq), pl.BlockSpec((1, page, W_HEADS), seq),
                pl.BlockSpec((1, page, W_HEADS), seq)]
    in_specs += _page_specs(n_pg, (N_HEADS, HEAD_DIM, page), layer) * 2
    grid_spec = pltpu.PrefetchScalarGridSpec(
        num_scalar_prefetch=1, grid=(bsz, n_blk // bps), in_specs=in_specs,
        out_specs=pl.BlockSpec((1, n_tok, W_HEADS), seq),
        scratch_shapes=[pltpu.VMEM((rows, LANES), F32), pltpu.VMEM((rows, LANES), F32),
                        pltpu.VMEM((rows, LANES), F32), pltpu.VMEM((n_blk, rows, W_HEADS), F32)])
    return pl.pallas_call(
        functools.partial(_moba_dec_kernel, n_pg=n_pg, n_blk=n_blk, bps=bps),
        grid_spec=grid_spec,
        out_shape=jax.ShapeDtypeStruct((bsz, n_tok, W_HEADS), F32),
        compiler_params=_cparams(("parallel", "arbitrary")),
    )(page_table, q, kn, vn, *([cache_kt] * n_pg), *([cache_vt] * n_pg))


def _rope_tables(pos):
    half = ROPE_DIM // 2
    inv = ROPE_THETA ** (-jnp.arange(half, dtype=F32) / half)
    ang = pos.astype(F32)[:, None] * inv[None, :]
    cos, sin = jnp.cos(ang), jnp.sin(ang)
    t = pos.shape[0]
    one = jnp.ones((t, HEAD_DIM - ROPE_DIM), F32)
    zero = jnp.zeros((t, HEAD_DIM - ROPE_DIM), F32)
    z8 = jnp.zeros((t, half), F32)
    cos_h = jnp.concatenate([cos, cos, one], axis=1)
    sp_h = jnp.concatenate([z8, sin, zero], axis=1)
    sm_h = jnp.concatenate([-sin, z8, zero], axis=1)
    tile2 = lambda a: jnp.concatenate([a, a], axis=1)
    return tile2(cos_h), tile2(sp_h), tile2(sm_h)


def _layer_params(l, p):
    d = p["w_in"].shape[1]
    w_in = p["w_in"][l]
    o = 0
    w_fox = w_in[:, o:o + 3 * W_HEADS]; o += 3 * W_HEADS
    w_ff = w_in[:, o:o + N_HEADS]; o += N_HEADS
    w_moba = w_in[:, o:o + 3 * W_HEADS]; o += 3 * W_HEADS
    w_ur = w_in[:, o:o + RWKV_COLS]; o += RWKV_COLS
    w_ug = w_in[:, o:o + N_BRANCH * d]
    pad_l = lambda a, n: jnp.pad(a, ((0, 0), (0, n - a.shape[1])))
    tile_h = lambda a: jnp.tile(a, N_HEADS)[None, :]
    zeros64 = jnp.zeros((64, W_HEADS), F32)
    return dict(
        norm_mix=p["norm_mix"][l][None, :],
        w_fox=w_fox.astype(BF16), w_ff=pad_l(w_ff, LANES).astype(BF16),
        b_ff=pad_l(p["b_forget"][l][None, :], LANES),
        w_moba=w_moba.astype(BF16), w_ur=w_ur.astype(BF16), w_ug=w_ug.astype(BF16),
        qn_fox=tile_h(p["qn_fox"][l]), kn_fox=tile_h(p["kn_fox"][l]),
        qn_moba=tile_h(p["qn_moba"][l]), kn_moba=tile_h(p["kn_moba"][l]),
        mu=p["rwkv_mu"][l][None, :], w0=p["rwkv_w0"][l][None, :], a0=p["rwkv_a0"][l][None, :],
        w_up=jnp.concatenate([p["rwkv_w_up"][l], zeros64], axis=0).astype(BF16),
        a_up=jnp.concatenate([zeros64, p["rwkv_a_up"][l]], axis=0).astype(BF16),
        g_up=p["rwkv_g_up"][l].astype(BF16),
        k_k=p["rwkv_k_k"][l][None, :], k_a=p["rwkv_k_a"][l][None, :],
        r_k=p["rwkv_r_k"][l].reshape(1, W_HEADS),
        lnx_w=p["rwkv_lnx_w"][l][None, :], lnx_b=p["rwkv_lnx_b"][l][None, :],
        w_br_fox=p["w_br_fox"][l].astype(BF16), w_br_rwkv=p["w_br_rwkv"][l].astype(BF16),
        w_br_moba=p["w_br_moba"][l].astype(BF16), w_out=p["w_out"][l].astype(BF16),
        norm_mlp=p["norm_mlp"][l][None, :],
        w_up_mlp=p["w_mlp_up"][l].astype(BF16), w_down_mlp=p["w_mlp_down"][l].astype(BF16),
    )


def _state_to_blockdiag(s):
    b = s.shape[0]
    st = jnp.swapaxes(s, -1, -2).reshape(b, N_PAIRS, 2, HEAD_DIM, HEAD_DIM)
    z = jnp.zeros_like(st[:, :, 0])
    top = jnp.concatenate([st[:, :, 0], z], axis=-1)
    bot = jnp.concatenate([z, st[:, :, 1]], axis=-1)
    return jnp.concatenate([top, bot], axis=-2)


def _blockdiag_to_state(sbd):
    b = sbd.shape[0]
    s0 = sbd[:, :, 0:HEAD_DIM, 0:HEAD_DIM]
    s1 = sbd[:, :, HEAD_DIM:, HEAD_DIM:]
    st = jnp.stack([s0, s1], axis=2).reshape(b, N_HEADS, HEAD_DIM, HEAD_DIM)
    return jnp.swapaxes(st, -1, -2)


def _mixers_common(x2, lp, gs_ones, place, rope, shift0, s0, bsz, t, tiles_per_seq,
                   fox_stack=None, moba_stack=None):
    m = x2.shape[0]
    qs, qe, qo, ke, ko, fk, fv, fvb, logf = _proj_fox(
        x2, lp["norm_mix"], lp["w_fox"], lp["w_ff"], lp["b_ff"], lp["qn_fox"], lp["kn_fox"],
        gs_ones, place, tiles_per_seq, stack=fox_stack)
    mq, mk, mv, mvb = _proj_moba(x2, lp["norm_mix"], lp["w_moba"], lp["qn_moba"], lp["kn_moba"],
                                 gs_ones, *rope, tiles_per_seq, stack=moba_stack)
    ur = _proj_plain(x2, lp["norm_mix"], lp["w_ur"])
    gates = _proj_plain(x2, lp["norm_mix"], lp["w_ug"], act="sigmoid")
    ur3 = ur.reshape(bsz, t, RWKV_COLS)
    u_prev = jnp.concatenate([shift0[:, None, :], ur3[:, :-1]], axis=1).reshape(m, RWKV_COLS)
    r, ld, k2, v, kk, b, bonus, g = _rwkv_prep(
        ur, u_prev, lp["mu"], lp["w0"], lp["w_up"], lp["a0"], lp["a_up"], lp["g_up"],
        lp["k_k"], lp["k_a"], lp["r_k"], gs_ones)
    t_pad = -(-t // RWKV_CHUNK) * RWKV_CHUNK
    seqs = [a.reshape(bsz, t, W_HEADS) for a in (r, ld, k2, v, kk, b)]
    if t_pad != t:
        seqs = [jnp.pad(a, ((0, 0), (0, t_pad - t), (0, 0))) for a in seqs]
    y_r, s_bd = _rwkv_scan(*seqs, _state_to_blockdiag(s0), nb=2 if bsz % 2 == 0 else 1)
    y_r = y_r[:, :t].reshape(m, W_HEADS)
    return dict(qs=qs, qe=qe, qo=qo, ke=ke, ko=ko, fk=fk, fv=fv, fvb=fvb, logf=logf,
                mq=mq, mk=mk, mv=mv, mvb=mvb, gates=gates,
                y_r=y_r, bonus=bonus, g=g, s_new=_blockdiag_to_state(s_bd), shift_new=ur3[:, -1])


def _finish_layer(x2, lp, gs_ones, pc, y_fox, y_moba):
    xo = _merge(x2, y_fox, y_moba, pc["y_r"], pc["bonus"], pc["g"], pc["gates"],
                lp["lnx_w"], lp["lnx_b"], gs_ones, lp["w_br_fox"], lp["w_br_rwkv"],
                lp["w_br_moba"], lp["w_out"])
    return _mlp(xo, lp["norm_mlp"], lp["w_up_mlp"], lp["w_down_mlp"])


def _flash_tile(t):
    for tile in (1024, 512, MOBA_BLOCK):
        if t % tile == 0:
            return tile
    raise ValueError("prompt length must be a multiple of the MoBA block")


def kernel(x_prompt, x_sample, cache_fox_k, cache_fox_v, cache_fox_logf, cache_moba_k, cache_moba_v, state_rwkv, state_rwkv_shift, page_table, norm_mix, w_in, b_forget, qn_fox, kn_fox, qn_moba, kn_moba, rwkv_mu, rwkv_w0, rwkv_w_up, rwkv_a0, rwkv_a_up, rwkv_g_up, rwkv_k_k, rwkv_k_a, rwkv_r_k, rwkv_lnx_w, rwkv_lnx_b, w_br_fox, w_br_rwkv, w_br_moba, w_out, norm_mlp, w_mlp_up, w_mlp_down):
    params = dict(norm_mix=norm_mix, w_in=w_in, b_forget=b_forget, qn_fox=qn_fox, kn_fox=kn_fox,
                  qn_moba=qn_moba, kn_moba=kn_moba, rwkv_mu=rwkv_mu, rwkv_w0=rwkv_w0,
                  rwkv_w_up=rwkv_w_up, rwkv_a0=rwkv_a0, rwkv_a_up=rwkv_a_up, rwkv_g_up=rwkv_g_up,
                  rwkv_k_k=rwkv_k_k, rwkv_k_a=rwkv_k_a, rwkv_r_k=rwkv_r_k, rwkv_lnx_w=rwkv_lnx_w,
                  rwkv_lnx_b=rwkv_lnx_b, w_br_fox=w_br_fox, w_br_rwkv=w_br_rwkv,
                  w_br_moba=w_br_moba, w_out=w_out, norm_mlp=norm_mlp, w_mlp_up=w_mlp_up,
                  w_mlp_down=w_mlp_down)
    depth = w_in.shape[0]
    bp, tp, d = x_prompt.shape
    bs, ts, _ = x_sample.shape
    n_pages = page_table.shape[1]
    page = cache_fox_k.shape[2]
    past_len = n_pages * page
    assert past_len % MOBA_BLOCK == 0 and ts <= MOBA_BLOCK and tp % MOBA_BLOCK == 0
    tile = _flash_tile(tp)
    tm = 256
    assert (bp * tp) % tm == 0 and tp % tm == 0 and (bs * ts) % tm == 0

    hid = jnp.arange(W_HEADS) // HEAD_DIM
    gs_ones = (hid[:, None] == hid[None, :]).astype(BF16)
    place = _bias_placement()
    rope_p = _rope_tables(jnp.arange(tp, dtype=jnp.int32))
    rope_s = tuple(jnp.tile(a, (bs, 1)) for a in _rope_tables(past_len + jnp.arange(ts, dtype=jnp.int32)))
    tview = lambda c: jnp.transpose(c, (0, 1, 3, 4, 2))
    fox_kt, fox_vt, moba_kt, moba_vt = (tview(c) for c in (cache_fox_k, cache_fox_v, cache_moba_k, cache_moba_v))
    fox_lft = jnp.transpose(cache_fox_logf, (0, 1, 3, 2))

    xp = x_prompt.reshape(bp * tp, d)
    xs = x_sample.reshape(bs * ts, d)
    rows_p = [[] for _ in range(3)]
    rows_s = [[] for _ in range(7)]
    fk_all, fv_all, mk_all, mv_all = (jnp.zeros((bp * depth * tp, W_HEADS), F32) for _ in range(4))
    for l in range(depth):
        lp = _layer_params(l, params)

        pc = _mixers_common(xp, lp, gs_ones, place, rope_p,
                            jnp.zeros((bp, RWKV_COLS), F32),
                            jnp.zeros((bp, N_HEADS, HEAD_DIM, HEAD_DIM), F32), bp, tp, tp // tm,
                            fox_stack=(fk_all, fv_all, l, depth), moba_stack=(mk_all, mv_all, l, depth))
        fk_all, fv_all, mk_all, mv_all = pc["fk"], pc["fv"], pc["mk"], pc["mv"]
        b3 = lambda a: a.reshape(bp, tp, W_HEADS)
        y_fox = _flash(b3(pc["qe"]), b3(pc["qo"]), b3(pc["ke"]), b3(pc["ko"]), b3(pc["fvb"]), tile)
        mqe, mqo, mke, mko = _moba_gate(b3(pc["mq"]), mk_all.reshape(bp, depth * tp, W_HEADS),
                                        l * (tp // MOBA_BLOCK))
        y_moba = _flash(mqe, mqo, mke, mko, b3(pc["mvb"]), tile)
        xp = _finish_layer(xp, lp, gs_ones, pc, y_fox.reshape(bp * tp, W_HEADS),
                           y_moba.reshape(bp * tp, W_HEADS))
        for acc, val in zip(rows_p, (pc["logf"].reshape(bp, tp, N_HEADS), pc["s_new"], pc["shift_new"])):
            acc.append(val)

        sc = _mixers_common(xs, lp, gs_ones, place, rope_s, state_rwkv_shift[:, l],
                            state_rwkv[:, l], bs, ts, 1)
        s3 = lambda a: a.reshape(bs, ts, W_HEADS)
        padt = lambda a: jnp.pad(s3(a), ((0, 0), (0, page - ts), (0, 0)))
        lfn = jnp.pad(jnp.swapaxes(sc["logf"].reshape(bs, ts, N_HEADS), 1, 2),
                      ((0, 0), (0, 0), (0, page - ts)))
        y_fox_s = _fox_decode(page_table, s3(sc["qs"]), padt(sc["fk"]), padt(sc["fv"]), lfn,
                              fox_kt, fox_vt, fox_lft, l)
        y_moba_s = _moba_decode(page_table, s3(sc["mq"]), padt(sc["mk"]), padt(sc["mv"]),
                                moba_kt, moba_vt, l)
        xs = _finish_layer(xs, lp, gs_ones, sc, y_fox_s.reshape(bs * ts, W_HEADS),
                           y_moba_s.reshape(bs * ts, W_HEADS))
        h5s = lambda a: a.reshape(bs, ts, N_HEADS, HEAD_DIM)
        for acc, val in zip(rows_s, (h5s(sc["fk"]), h5s(sc["fv"]), sc["logf"].reshape(bs, ts, N_HEADS),
                                     h5s(sc["mk"]), h5s(sc["mv"]), sc["s_new"], sc["shift_new"])):
            acc.append(val)

    p_logf, p_rwkv, p_shift = (jnp.stack(a, axis=1) for a in rows_p)
    kv5 = lambda a: a.reshape(bp, depth, tp, N_HEADS, HEAD_DIM)
    outs_s = [jnp.stack(a, axis=1) for a in rows_s]
    return (xp.reshape(bp, tp, d), xs.reshape(bs, ts, d), kv5(fk_all), kv5(fv_all), p_logf,
            kv5(mk_all), kv5(mv_all), p_rwkv, p_shift, *outs_s)
```

```python
import functools
import math

import numpy as np
import jax
import jax.numpy as jnp
from jax import lax
from jax.experimental import pallas as pl
from jax.experimental.pallas import tpu as pltpu

F32 = jnp.float32
BF16 = jnp.bfloat16

HEAD_DIM = 64
N_HEADS = 8
W_HEADS = N_HEADS * HEAD_DIM
N_PAIRS = N_HEADS // 2
RWKV_COLS = 3 * W_HEADS + 64 + 64 + 128
N_BRANCH = 3
MOBA_BLOCK = 256
MOBA_TOPK = 3
ROPE_THETA = 500000.0
ROPE_DIM = HEAD_DIM // 4
NORM_EPS = 1e-6
LNX_EPS = 64e-5
L2_EPS = 1e-12
LOG2E = 1.4426950408889634
Q_SCALE = HEAD_DIM ** -0.5 * LOG2E

LANES = 128
SUBLANES = 8
VMEM_LIMIT = 56 * 1024 * 1024

NEG = -1e30
BIAS_OFF = -(2.0 ** 100)
N_BIAS_LANES = 3
RWKV_CHUNK = 64


def _cparams(sem):
    return pltpu.CompilerParams(dimension_semantics=sem, vmem_limit_bytes=VMEM_LIMIT)


def _dot(a, b):
    return jnp.dot(a, b, preferred_element_type=F32)


def _dot_nt(a, b):
    return lax.dot_general(a, b, (((1,), (1,)), ((), ())), preferred_element_type=F32)


def _split2(x):
    hi = x.astype(BF16)
    lo = (x - hi.astype(F32)).astype(BF16)
    return hi, lo


def _split3(x):
    hi = x.astype(BF16)
    r1 = x - hi.astype(F32)
    mid = r1.astype(BF16)
    lo = (r1 - mid.astype(F32)).astype(BF16)
    return hi, mid, lo


def _dot1(a, b, dot=_dot):
    return dot(a.astype(BF16), b.astype(BF16))


def _dot3(a, b, dot=_dot):
    ah, al = _split2(a)
    bh, bl = _split2(b)
    return dot(ah, bh) + (dot(ah, bl) + dot(al, bh))


def _dot_exact_lhs(a_bf16, b, dot=_dot):
    h, m, l = _split3(b)
    return dot(a_bf16, h) + (dot(a_bf16, m) + dot(a_bf16, l))


def _head_reduce(x, g_bf16):
    hi, lo = _split2(x)
    return _dot(hi, g_bf16) + _dot(lo, g_bf16)


def _softplus(z):
    return jnp.maximum(z, 0.0) + jnp.log(1.0 + jnp.exp(-jnp.abs(z)))


def _sigmoid(z):
    return 1.0 / (1.0 + jnp.exp(-z))


def _rms(x, g):
    ms = jnp.mean(x * x, axis=-1, keepdims=True)
    return x * lax.rsqrt(ms + NORM_EPS) * g


def _iota(shape, dim):
    return lax.broadcasted_iota(jnp.int32, shape, dim)


def _lane_chunks(s):
    return [s[:, j * LANES:(j + 1) * LANES] for j in range(s.shape[1] // LANES)]


def _softmax_update(s, m_prev, l_prev):
    ch = _lane_chunks(s)
    mx = ch[0]
    for c in ch[1:]:
        mx = jnp.maximum(mx, c)
    m_new = jnp.maximum(m_prev, jnp.max(mx, axis=-1, keepdims=True))
    alpha = jnp.exp2(m_prev - m_new)
    ps = [jnp.exp2(c - m_new) for c in ch]
    lsum = ps[0]
    for p_ in ps[1:]:
        lsum = lsum + p_
    pe = ps[0].astype(BF16) if len(ps) == 1 else jnp.concatenate([p_.astype(BF16) for p_ in ps], axis=1)
    return m_new, alpha, alpha * l_prev + lsum, pe


def _proj_plain_kernel(x_ref, g_ref, w_ref, o_ref, *, act):
    h = _rms(x_ref[...], g_ref[...]).astype(BF16)
    u = _dot(h, w_ref[...])
    if act == "sigmoid":
        u = _sigmoid(u)
    o_ref[...] = u


def _proj_plain(x, g, w, act=None, tm=256):
    m, d = x.shape
    n = w.shape[1]
    return pl.pallas_call(
        functools.partial(_proj_plain_kernel, act=act),
        grid=(m // tm,),
        in_specs=[pl.BlockSpec((tm, d), lambda i: (i, 0)),
                  pl.BlockSpec((1, d), lambda i: (0, 0)),
                  pl.BlockSpec((d, n), lambda i: (0, 0))],
        out_specs=pl.BlockSpec((tm, n), lambda i: (i, 0)),
        out_shape=jax.ShapeDtypeStruct((m, n), F32),
        compiler_params=_cparams(("parallel",)),
    )(x, g, w)


def _proj_fox_kernel(x_ref, g_ref, w_ref, wf_ref, bf_ref, qn_ref, kn_ref, gm_ref, place_ref, *rest,
                     tiles_per_seq, n_alias):
    qs_ref, qe_ref, qo_ref, ke_ref, ko_ref, k_ref, v_ref, vb_ref, lf_ref, carry_scr = rest[n_alias:]
    i = pl.program_id(0)
    tm = x_ref.shape[0]
    h = _rms(x_ref[...], g_ref[...]).astype(BF16)
    u = _dot(h, w_ref[...])
    gm = gm_ref[...]
    fq = u[:, 0:W_HEADS]
    fk = u[:, W_HEADS:2 * W_HEADS]
    q_ms = _head_reduce(fq * fq, gm) * (1.0 / HEAD_DIM)
    k_ms = _head_reduce(fk * fk, gm) * (1.0 / HEAD_DIM)
    q = fq * lax.rsqrt(q_ms + NORM_EPS) * qn_ref[...] * Q_SCALE
    k = fk * lax.rsqrt(k_ms + NORM_EPS) * kn_ref[...]
    v = u[:, 2 * W_HEADS:3 * W_HEADS]
    k_ref[...] = k
    v_ref[...] = v
    vb_ref[...] = v.astype(BF16)
    z = _dot(h, wf_ref[...]) + bf_ref[...]
    lane = _iota((1, LANES), 1)
    lf = jnp.where(lane < N_HEADS, -_softplus(-z), 0.0)
    lf_ref[...] = lf[:, 0:N_HEADS]

    @pl.when(i % tiles_per_seq == 0)
    def _():
        carry_scr[...] = jnp.zeros_like(carry_scr)

    tri = (_iota((tm, tm), 0) >= _iota((tm, tm), 1)).astype(BF16)
    c = _dot_exact_lhs(tri, lf) + carry_scr[...]
    carry_scr[...] = c[tm - 1:tm, :]

    lane512 = _iota((1, W_HEADS), 1)
    even = (lane512 // HEAD_DIM) % 2 == 0
    lt = lane512 % LANES
    ones_e = jnp.where(jnp.logical_and(lt >= HEAD_DIM, lt < HEAD_DIM + N_BIAS_LANES), 1.0, 0.0)
    ones_o = jnp.where(lt < N_BIAS_LANES, 1.0, 0.0)
    qs_ref[...] = q.astype(BF16)
    qe_ref[...] = jnp.where(even, q, ones_e).astype(BF16)
    qo_ref[...] = jnp.where(even, ones_o, q).astype(BF16)
    pieces = _split3(c * LOG2E)
    kb_e = _dot(pieces[0], place_ref[0]) + _dot(pieces[1], place_ref[1]) + _dot(pieces[2], place_ref[2])
    kb_o = _dot(pieces[0], place_ref[3]) + _dot(pieces[1], place_ref[4]) + _dot(pieces[2], place_ref[5])
    ke_ref[...] = (jnp.where(even, k, 0.0) + kb_e).astype(BF16)
    ko_ref[...] = (jnp.where(even, 0.0, k) + kb_o).astype(BF16)


def _bias_placement():
    pm = np.zeros((2 * N_BIAS_LANES, LANES, W_HEADS), np.float32)
    for h in range(N_HEADS):
        p = h // 2
        for j in range(N_BIAS_LANES):
            if h % 2 == 0:
                pm[j, h, p * LANES + HEAD_DIM + j] = -1.0
            else:
                pm[N_BIAS_LANES + j, h, p * LANES + j] = -1.0
    return jnp.asarray(pm, BF16)


def _stack_rows(stack, tiles_per_seq, tm):
    k_st, v_st, layer, depth, n_rows = stack
    srow = lambda i: (((i // tiles_per_seq) * depth + layer) * tiles_per_seq + i % tiles_per_seq, 0)
    return (pl.BlockSpec((tm, W_HEADS), srow), jax.ShapeDtypeStruct((n_rows, W_HEADS), F32),
            [k_st, v_st], [pl.BlockSpec(memory_space=pl.ANY)] * 2)


def _proj_fox(x, g, w, wf, bfg, qn, kn, gm, place, tiles_per_seq, tm=256, stack=None):
    m, d = x.shape
    row = lambda i: (i, 0)
    fix = lambda i: (0, 0)
    act = pl.BlockSpec((tm, W_HEADS), row)
    bf_out = jax.ShapeDtypeStruct((m, W_HEADS), BF16)
    kv_spec, kv_shape, extra, extra_specs, aliases = act, jax.ShapeDtypeStruct((m, W_HEADS), F32), [], [], {}
    if stack is not None:
        kv_spec, kv_shape, extra, extra_specs = _stack_rows(stack, tiles_per_seq, tm)
        aliases = {9: 5, 10: 6} if extra else {}
    return pl.pallas_call(
        functools.partial(_proj_fox_kernel, tiles_per_seq=tiles_per_seq, n_alias=len(extra)),
        grid=(m // tm,),
        in_specs=[pl.BlockSpec((tm, d), row), pl.BlockSpec((1, d), fix),
                  pl.BlockSpec((d, 3 * W_HEADS), fix), pl.BlockSpec((d, LANES), fix),
                  pl.BlockSpec((1, LANES), fix), pl.BlockSpec((1, W_HEADS), fix),
                  pl.BlockSpec((1, W_HEADS), fix), pl.BlockSpec((W_HEADS, W_HEADS), fix),
                  pl.BlockSpec((2 * N_BIAS_LANES, LANES, W_HEADS), lambda i: (0, 0, 0))] + extra_specs,
        out_specs=[act] * 5 + [kv_spec, kv_spec, act, pl.BlockSpec((tm, N_HEADS), row)],
        out_shape=[bf_out] * 5 + [kv_shape, kv_shape, bf_out, jax.ShapeDtypeStruct((m, N_HEADS), F32)],
        scratch_shapes=[pltpu.VMEM((1, LANES), F32)],
        input_output_aliases=aliases,
        compiler_params=_cparams(("arbitrary",)),
    )(x, g, w, wf, bfg, qn, kn, gm, place, *extra)


def _rotary(x, cos, sp, sm):
    outs = []
    for j in range(W_HEADS // LANES):
        xs = x[:, j * LANES:(j + 1) * LANES]
        outs.append(xs * cos + pltpu.roll(xs, ROPE_DIM // 2, 1) * sp
                    + pltpu.roll(xs, LANES - ROPE_DIM // 2, 1) * sm)
    return jnp.concatenate(outs, axis=1)


def _proj_moba_kernel(x_ref, g_ref, w_ref, qn_ref, kn_ref, gm_ref, cos_ref, sp_ref, sm_ref, *rest, n_alias):
    q_ref, k_ref, v_ref, vb_ref = rest[n_alias:]
    h = _rms(x_ref[...], g_ref[...]).astype(BF16)
    u = _dot(h, w_ref[...])
    gm = gm_ref[...]
    mq = u[:, 0:W_HEADS]
    mk = u[:, W_HEADS:2 * W_HEADS]
    q_ms = _head_reduce(mq * mq, gm) * (1.0 / HEAD_DIM)
    k_ms = _head_reduce(mk * mk, gm) * (1.0 / HEAD_DIM)
    cos, sp, sm = cos_ref[...], sp_ref[...], sm_ref[...]
    q_ref[...] = _rotary(mq * lax.rsqrt(q_ms + NORM_EPS) * qn_ref[...], cos, sp, sm)
    k_ref[...] = _rotary(mk * lax.rsqrt(k_ms + NORM_EPS) * kn_ref[...], cos, sp, sm)
    v = u[:, 2 * W_HEADS:3 * W_HEADS]
    v_ref[...] = v
    vb_ref[...] = v.astype(BF16)


def _proj_moba(x, g, w, qn, kn, gm, cos, sp, sm, tiles_per_seq, tm=256, stack=None):
    m, d = x.shape
    ntab = cos.shape[0] // tm
    row = lambda i: (i, 0)
    fix = lambda i: (0, 0)
    tab = lambda i: (i % ntab, 0)
    act = pl.BlockSpec((tm, W_HEADS), row)
    f_out = jax.ShapeDtypeStruct((m, W_HEADS), F32)
    kv_spec, kv_shape, extra, extra_specs, aliases = act, f_out, [], [], {}
    if stack is not None:
        kv_spec, kv_shape, extra, extra_specs = _stack_rows(stack, tiles_per_seq, tm)
        aliases = {9: 1, 10: 2} if extra else {}
    return pl.pallas_call(
        functools.partial(_proj_moba_kernel, n_alias=len(extra)),
        grid=(m // tm,),
        in_specs=[pl.BlockSpec((tm, d), row), pl.BlockSpec((1, d), fix),
                  pl.BlockSpec((d, 3 * W_HEADS), fix), pl.BlockSpec((1, W_HEADS), fix),
                  pl.BlockSpec((1, W_HEADS), fix), pl.BlockSpec((W_HEADS, W_HEADS), fix),
                  pl.BlockSpec((tm, LANES), tab), pl.BlockSpec((tm, LANES), tab),
                  pl.BlockSpec((tm, LANES), tab)] + extra_specs,
        out_specs=[act, kv_spec, kv_spec, act],
        out_shape=[f_out, kv_shape, kv_shape, jax.ShapeDtypeStruct((m, W_HEADS), BF16)],
        input_output_aliases=aliases,
        compiler_params=_cparams(("parallel",)),
    )(x, g, w, qn, kn, gm, cos, sp, sm, *extra)


def _flash_kernel(qt_ref, kt_ref, qe_ref, qo_ref, ke_ref, ko_ref, v_ref, o_ref, m_scr, l_scr, acc_scr,
                  *, tq, tk):
    s_idx = pl.program_id(2)
    qi = qt_ref[s_idx]
    ki = kt_ref[s_idx]

    @pl.when(ki == 0)
    def _():
        m_scr[...] = jnp.full_like(m_scr, NEG)
        l_scr[...] = jnp.zeros_like(l_scr)
        acc_scr[...] = jnp.zeros_like(acc_scr)

    def body(diagonal):
        v = v_ref[0]
        for h, (q_ref, k_ref) in enumerate(((qe_ref, ke_ref), (qo_ref, ko_ref))):
            s = _dot_nt(q_ref[0], k_ref[0])
            if diagonal:
                s = jnp.where(_iota((tq, tk), 1) <= _iota((tq, tk), 0), s, NEG)
            m_new, alpha, l_new, pe = _softmax_update(s, m_scr[h], l_scr[h])
            l_scr[h] = l_new
            acc_scr[h] = alpha * acc_scr[h] + _dot(pe, v)
            m_scr[h] = m_new

    @pl.when(ki < qi)
    def _():
        body(False)

    @pl.when(ki == qi)
    def _():
        body(True)
        lane = _iota((1, LANES), 1)
        l0 = jnp.sum(l_scr[0], axis=-1, keepdims=True)
        l1 = jnp.sum(l_scr[1], axis=-1, keepdims=True)
        o_ref[0] = jnp.where(lane < HEAD_DIM, acc_scr[0] / l0, acc_scr[1] / l1)


def _tri_tables(nq):
    qt = np.concatenate([np.full(i + 1, i, np.int32) for i in range(nq)])
    kt = np.concatenate([np.arange(i + 1, dtype=np.int32) for i in range(nq)])
    return jnp.asarray(qt), jnp.asarray(kt)


def _flash(qe, qo, ke, ko, v, tile):
    b, t, _ = v.shape
    tq = tk = tile
    qt, kt = _tri_tables(t // tq)
    qmap = lambda bb, p, s, qt_, kt_: (bb, qt_[s], p)
    kmap = lambda bb, p, s, qt_, kt_: (bb, kt_[s], p)
    grid_spec = pltpu.PrefetchScalarGridSpec(
        num_scalar_prefetch=2, grid=(b, N_PAIRS, int(qt.shape[0])),
        in_specs=[pl.BlockSpec((1, tq, LANES), qmap)] * 2 + [pl.BlockSpec((1, tk, LANES), kmap)] * 3,
        out_specs=pl.BlockSpec((1, tq, LANES), qmap),
        scratch_shapes=[pltpu.VMEM((2, tq, LANES), F32)] * 3)
    return pl.pallas_call(
        functools.partial(_flash_kernel, tq=tq, tk=tk),
        grid_spec=grid_spec,
        out_shape=jax.ShapeDtypeStruct((b, t, W_HEADS), F32),
        compiler_params=_cparams(("parallel", "parallel", "arbitrary")),
    )(qt, kt, qe, qo, ke, ko, v)


def _top3_select(gm, n_idx, axis=-1):
    sel = jnp.zeros(gm.shape, jnp.bool_)
    big = jnp.int32(1 << 20)
    for _ in range(MOBA_TOPK):
        mx = jnp.max(gm, axis=axis, keepdims=True)
        is_max = jnp.logical_and(gm == mx, mx > 0.5 * NEG)
        idx = jnp.min(jnp.where(is_max, n_idx, big), axis=axis, keepdims=True)
        pick = n_idx == idx
        sel = jnp.logical_or(sel, pick)
        gm = jnp.where(pick, NEG, gm)
    return sel


def _moba_gate_kernel(q_ref, k_ref, qe_ref, qo_ref, ke_ref, ko_ref, kme_scr, kmo_scr, *, n_blk):
    i = pl.program_id(1)
    rows = q_ref.shape[1]

    @pl.when(i == 0)
    def _():
        kme_scr[...] = jnp.zeros_like(kme_scr)
        kmo_scr[...] = jnp.zeros_like(kmo_scr)

    q = q_ref[0]
    k = k_ref[0]
    lane512 = _iota((1, W_HEADS), 1)
    lane = _iota((1, LANES), 1)
    for p in range(N_PAIRS):
        sl = slice(p * LANES, (p + 1) * LANES)
        qpair = q[:, sl] * Q_SCALE
        kpair = k[:, sl]
        for h in range(2):
            head = 2 * p + h
            qm = jnp.where(lane512 // HEAD_DIM == head, q, 0.0)
            km = kme_scr[...] if h == 0 else kmo_scr[...]
            g_t = _dot3(km, qm, _dot_nt)
            base = HEAD_DIM if h == 0 else 0
            slot = _iota((LANES, 1), 0) - base
            valid_t = jnp.logical_and(slot >= 0, slot < i)
            sel_t = _top3_select(jnp.where(valid_t, g_t, NEG), slot, axis=0)
            keep_t = jnp.logical_or(jnp.logical_or(sel_t, slot == i), slot >= n_blk)
            bias = jnp.transpose(jnp.where(keep_t, 0.0, BIAS_OFF))
            n_idx = lane - base
            in_half = jnp.logical_and(n_idx >= 0, n_idx < HEAD_DIM)
            qa = jnp.where(in_half, bias, qpair).astype(BF16)
            ka = jnp.where(in_half, jnp.where(n_idx == i, 1.0, 0.0), kpair).astype(BF16)
            if h == 0:
                qe_ref[0, :, sl] = qa
                ke_ref[0, :, sl] = ka
            else:
                qo_ref[0, :, sl] = qa
                ko_ref[0, :, sl] = ka

    kmean = jnp.sum(k, axis=0, keepdims=True) * (1.0 / rows)
    kme_scr[pl.ds(HEAD_DIM + i, 1), :] = kmean
    kmo_scr[pl.ds(i, 1), :] = kmean


def _moba_gate(q, k, k_blk_off=0):
    b, t, _ = q.shape
    n_blk = t // MOBA_BLOCK
    assert n_blk <= HEAD_DIM
    blk = lambda bb, i: (bb, i, 0)
    return pl.pallas_call(
        functools.partial(_moba_gate_kernel, n_blk=n_blk),
        grid=(b, n_blk),
        in_specs=[pl.BlockSpec((1, MOBA_BLOCK, W_HEADS), blk),
                  pl.BlockSpec((1, MOBA_BLOCK, W_HEADS), lambda bb, i: (bb, k_blk_off + i, 0))],
        out_specs=[pl.BlockSpec((1, MOBA_BLOCK, W_HEADS), blk)] * 4,
        out_shape=[jax.ShapeDtypeStruct((b, t, W_HEADS), BF16)] * 4,
        scratch_shapes=[pltpu.VMEM((LANES, W_HEADS), F32)] * 2,
        compiler_params=_cparams(("parallel", "arbitrary")),
    )(q, k)


def _rwkv_prep_kernel(u_ref, up_ref, sh_ref, mu_ref, w0_ref, wup_ref, a0_ref, aup_ref, gup_ref,
                      kk_ref_w, ka_ref, rk_ref, gs_ref,
                      r_o, ld_o, k_o, v_o, kk_o, b_o, bonus_o, g_o, *, tiles_per_seq):
    u = u_ref[...]
    if tiles_per_seq is None:
        u_prev = up_ref[...]
    else:
        first = pl.program_id(0) % tiles_per_seq == 0
        row0 = jnp.where(first, sh_ref[0], up_ref[SUBLANES - 1:SUBLANES, :])
        u_prev = jnp.where(_iota(u.shape, 0) == 0, row0, pltpu.roll(u, 1, 0))
    um = u + (u_prev - u) * mu_ref[...]
    r = um[:, 0:W_HEADS]
    k = um[:, W_HEADS:2 * W_HEADS]
    v = um[:, 2 * W_HEADS:3 * W_HEADS]
    wa = um[:, 3 * W_HEADS:3 * W_HEADS + LANES]
    gd = um[:, 3 * W_HEADS + LANES:3 * W_HEADS + 2 * LANES]
    w_log = -_softplus(-(w0_ref[...] + _dot(jnp.tanh(wa).astype(BF16), wup_ref[...]))) - 0.5
    ld_o[...] = -jnp.exp(w_log)
    a = _sigmoid(a0_ref[...] + _dot(wa.astype(BF16), aup_ref[...]))
    g_o[...] = _dot(_sigmoid(gd).astype(BF16), gup_ref[...])
    gs = gs_ref[...]
    kx = k * kk_ref_w[...]
    kk = kx * lax.rsqrt(_head_reduce(kx * kx, gs) + L2_EPS)
    k2 = k * (1.0 + (a - 1.0) * ka_ref[...])
    r_o[...] = r
    k_o[...] = k2
    v_o[...] = v
    kk_o[...] = kk
    b_o[...] = kk * a
    bonus_o[...] = _head_reduce(r * k2 * rk_ref[...], gs) * v


def _rwkv_prep(u, shift0, t, mu, w0, wup, a0, aup, gup, k_k, k_a, r_k, gs, tm=256):
    m = u.shape[0]
    bsz = m // t
    row = lambda i: (i, 0)
    fix = lambda i: (0, 0)
    vec = pl.BlockSpec((1, W_HEADS), fix)
    lr = pl.BlockSpec((LANES, W_HEADS), fix)
    if t % tm == 0:
        tiles_per_seq = t // tm
        up = u
        up_spec = pl.BlockSpec((SUBLANES, RWKV_COLS),
                               lambda i: (jnp.maximum(i * (tm // SUBLANES) - 1, 0), 0))
        sh = shift0[:, None, :]
        sh_spec = pl.BlockSpec((1, 1, RWKV_COLS), lambda i: (i // tiles_per_seq, 0, 0))
    else:
        tiles_per_seq = None
        u3 = u.reshape(bsz, t, RWKV_COLS)
        up = jnp.concatenate([shift0[:, None, :], u3[:, :-1]], axis=1).reshape(m, RWKV_COLS)
        up_spec = pl.BlockSpec((tm, RWKV_COLS), row)
        sh = shift0[0:1, None, :]
        sh_spec = pl.BlockSpec((1, 1, RWKV_COLS), lambda i: (0, 0, 0))
    return pl.pallas_call(
        functools.partial(_rwkv_prep_kernel, tiles_per_seq=tiles_per_seq),
        grid=(m // tm,),
        in_specs=[pl.BlockSpec((tm, RWKV_COLS), row), up_spec, sh_spec,
                  pl.BlockSpec((1, RWKV_COLS), fix), vec, lr, vec, lr, lr, vec, vec, vec,
                  pl.BlockSpec((W_HEADS, W_HEADS), fix)],
        out_specs=[pl.BlockSpec((tm, W_HEADS), row)] * 8,
        out_shape=[jax.ShapeDtypeStruct((m, W_HEADS), F32)] * 8,
        compiler_params=_cparams(("parallel",)),
    )(u, up, sh, mu, w0, wup, a0, aup, gup, k_k, k_a, r_k, gs)


def _rwkv_chunks(chains, consts, cc, n_double):
    tri, m0, strict, incl, eye = consts

    def stack(x):
        return jnp.concatenate([jnp.where(m0, x, 0.0), jnp.where(m0, 0.0, x)], axis=0)

    cums = [_dot_exact_lhs(tri, ch[1]) for ch in chains]
    pre = []
    for (r, ld, k, v, kk, b, st), cum in zip(chains, cums):
        g_in = jnp.exp(cum)
        g_inv = jnp.exp(-cum)
        g_end = jnp.exp(cum[cc - 1:cc, :] - cum)
        ar = jnp.concatenate([stack(-kk * jnp.exp(cum - ld)), stack(r * g_in)], axis=0)
        pre.append(dict(ar=ar, b2=stack(b * g_inv), k2=stack(k * g_inv), v2=stack(v),
                        bg2t=jnp.transpose(stack(b * g_end)), kg2t=jnp.transpose(stack(k * g_end)),
                        g_col=jnp.transpose(jnp.broadcast_to(g_in[cc - 1:cc, :], (LANES, LANES))), st=st))
    mbs = [_dot3(c["ar"], c["b2"], _dot_nt) for c in pre]
    mks = [_dot1(c["ar"], c["k2"], _dot_nt) for c in pre]
    arss = [_dot1(c["ar"], c["st"]) for c in pre]
    m_abs = [jnp.where(strict, mb[0:2 * cc], 0.0) for mb in mbs]
    t_invs = [eye + m for m in m_abs]
    pws = m_abs
    for _ in range(n_double - 1):
        pws = [_dot3(pw, pw) for pw in pws]
        t_invs = [t + _dot3(pw, t) for pw, t in zip(pws, t_invs)]
    rhs = [ars[0:2 * cc] + _dot1(jnp.where(strict, mk[0:2 * cc], 0.0), c["v2"])
           for ars, mk, c in zip(arss, mks, pre)]
    z2s = [_dot3(t, x) for t, x in zip(t_invs, rhs)]
    outs = []
    for c, mb, mk, ars, z2 in zip(pre, mbs, mks, arss, z2s):
        n_rb = jnp.where(incl, mb[2 * cc:4 * cc], 0.0)
        n_rk = jnp.where(incl, mk[2 * cc:4 * cc], 0.0)
        y2 = ars[2 * cc:4 * cc] + _dot1(n_rb, z2) + _dot1(n_rk, c["v2"])
        st_new = c["st"] * c["g_col"] + _dot1(c["bg2t"], z2) + _dot1(c["kg2t"], c["v2"])
        outs.append((y2[0:cc] + y2[cc:2 * cc], st_new))
    return outs


def _rwkv_scan_kernel(r_ref, ld_ref, k_ref, v_ref, kk_ref, b_ref, s0_ref, y_ref, sT_ref, st_scr,
                      *, chunk, n_double, nb):
    c = pl.program_id(1)
    cc = chunk

    @pl.when(c == 0)
    def _():
        st_scr[...] = s0_ref[...]

    rr = _iota((2 * cc, 2 * cc), 0)
    cl = _iota((2 * cc, 2 * cc), 1)
    same = (rr // cc) == (cl // cc)
    consts = ((_iota((cc, cc), 0) >= _iota((cc, cc), 1)).astype(BF16),
              _iota((cc, LANES), 1) < HEAD_DIM,
              jnp.logical_and(same, (rr % cc) > (cl % cc)),
              jnp.logical_and(same, (rr % cc) >= (cl % cc)),
              jnp.where(rr == cl, 1.0, 0.0))
    chains = [(bi, p, slice(p * LANES, (p + 1) * LANES)) for bi in range(nb) for p in range(N_PAIRS)]
    loaded = [tuple(x[bi, :, sl] for x in (r_ref, ld_ref, k_ref, v_ref, kk_ref, b_ref)) + (st_scr[bi, p],)
              for bi, p, sl in chains]
    results = _rwkv_chunks(loaded, consts, cc, n_double)
    for (bi, p, sl), (y, st_new) in zip(chains, results):
        y_ref[bi, :, sl] = y
        st_scr[bi, p] = st_new

    @pl.when(c == pl.num_programs(1) - 1)
    def _():
        sT_ref[...] = st_scr[...]


def _rwkv_scan(r, ld, k, v, kk, b, s0_bd, nb):
    bsz, t, _ = r.shape
    cc = RWKV_CHUNK
    assert t % cc == 0 and bsz % nb == 0
    blk = lambda i, c: (i, c, 0)
    st = lambda i, c: (i, 0, 0, 0)
    return pl.pallas_call(
        functools.partial(_rwkv_scan_kernel, chunk=cc, n_double=int(math.log2(cc)), nb=nb),
        grid=(bsz // nb, t // cc),
        in_specs=[pl.BlockSpec((nb, cc, W_HEADS), blk)] * 6
                 + [pl.BlockSpec((nb, N_PAIRS, LANES, LANES), st)],
        out_specs=[pl.BlockSpec((nb, cc, W_HEADS), blk), pl.BlockSpec((nb, N_PAIRS, LANES, LANES), st)],
        out_shape=[jax.ShapeDtypeStruct((bsz, t, W_HEADS), F32),
                   jax.ShapeDtypeStruct((bsz, N_PAIRS, LANES, LANES), F32)],
        scratch_shapes=[pltpu.VMEM((nb, N_PAIRS, LANES, LANES), F32)],
        compiler_params=_cparams(("parallel", "arbitrary")),
    )(r, ld, k, v, kk, b, s0_bd)


def _merge_kernel(x_ref, yf_ref, ym_ref, yr_ref, bonus_ref, g_ref, gates_ref,
                  lw_ref, lb_ref, gm_ref, wf_ref, wr_ref, wm_ref, wo_ref, o_ref):
    d = x_ref.shape[1]
    gm = gm_ref[...]
    yr = yr_ref[...]
    mean = _head_reduce(yr, gm) * (1.0 / HEAD_DIM)
    dv = yr - mean
    var = _head_reduce(dv * dv, gm) * (1.0 / HEAD_DIM)
    yn = dv * lax.rsqrt(var + LNX_EPS) * lw_ref[...] + lb_ref[...]
    y_rwkv = (yn + bonus_ref[...]) * g_ref[...]
    gates = gates_ref[...]
    merged = (gates[:, 0:d] * _dot(yf_ref[...].astype(BF16), wf_ref[...])
              + gates[:, d:2 * d] * _dot(y_rwkv.astype(BF16), wr_ref[...])
              + gates[:, 2 * d:3 * d] * _dot(ym_ref[...].astype(BF16), wm_ref[...]))
    o_ref[...] = x_ref[...] + _dot(merged.astype(BF16), wo_ref[...])


def _merge(x, yf, ym, yr, bonus, g, gates, lw, lb, gm, wf, wr, wm, wo, tm=256):
    m, d = x.shape
    row = lambda i: (i, 0)
    fix = lambda i: (0, 0)
    act = pl.BlockSpec((tm, W_HEADS), row)
    vec = pl.BlockSpec((1, W_HEADS), fix)
    wbr = pl.BlockSpec((W_HEADS, d), fix)
    return pl.pallas_call(
        _merge_kernel,
        grid=(m // tm,),
        in_specs=[pl.BlockSpec((tm, d), row), act, act, act, act, act,
                  pl.BlockSpec((tm, N_BRANCH * d), row), vec, vec,
                  pl.BlockSpec((W_HEADS, W_HEADS), fix), wbr, wbr, wbr, pl.BlockSpec((d, d), fix)],
        out_specs=pl.BlockSpec((tm, d), row),
        out_shape=jax.ShapeDtypeStruct((m, d), F32),
        compiler_params=_cparams(("parallel",)),
    )(x, yf, ym, yr, bonus, g, gates, lw, lb, gm, wf, wr, wm, wo)


def _mlp_kernel(x_ref, g_ref, wu_ref, wd_ref, o_ref, h_scr, acc_scr):
    f = pl.program_id(1)

    @pl.when(f == 0)
    def _():
        h_scr[...] = _rms(x_ref[...], g_ref[...]).astype(BF16)
        acc_scr[...] = x_ref[...]

    a = jnp.maximum(_dot(h_scr[...], wu_ref[...]), 0.0)
    acc_scr[...] += _dot((a * a).astype(BF16), wd_ref[...])

    @pl.when(f == pl.num_programs(1) - 1)
    def _():
        o_ref[...] = acc_scr[...]


def _mlp(x, g, wu, wd, tf=2048):
    m, d = x.shape
    tm = 512 if m % 512 == 0 else 256
    dff = wu.shape[1]
    return pl.pallas_call(
        _mlp_kernel,
        grid=(m // tm, dff // tf),
        in_specs=[pl.BlockSpec((tm, d), lambda i, f: (i, 0)), pl.BlockSpec((1, d), lambda i, f: (0, 0)),
                  pl.BlockSpec((d, tf), lambda i, f: (0, f)), pl.BlockSpec((tf, d), lambda i, f: (f, 0))],
        out_specs=pl.BlockSpec((tm, d), lambda i, f: (i, 0)),
        out_shape=jax.ShapeDtypeStruct((m, d), F32),
        scratch_shapes=[pltpu.VMEM((tm, d), BF16), pltpu.VMEM((tm, d), F32)],
        compiler_params=_cparams(("parallel", "arbitrary")),
    )(x, g, wu, wd)


def _qbd_rows(q8):
    rows = jnp.concatenate([jnp.broadcast_to(q8[i:i + 1], (N_HEADS, W_HEADS))
                            for i in range(q8.shape[0])], axis=0)
    keep = (_iota(rows.shape, 1) // HEAD_DIM) == (_iota(rows.shape, 0) % N_HEADS)
    return jnp.where(keep, rows, 0.0)


def _rows_to_tokens(o, n_tok):
    keep = (_iota(o.shape, 1) // HEAD_DIM) == (_iota(o.shape, 0) % N_HEADS)
    return jnp.sum(jnp.where(keep, o, 0.0).reshape(n_tok, N_HEADS, W_HEADS), axis=1)


def _page_matrix(refs):
    mats = [r[0, 0].reshape(W_HEADS, r.shape[-1]).astype(BF16) for r in refs]
    return mats[0] if len(mats) == 1 else jnp.concatenate(mats, axis=1)


def _lane_prefix(x):
    lane = _iota(x.shape, 1)
    sh = 1
    while sh < x.shape[1]:
        x = x + jnp.where(lane >= sh, pltpu.roll(x, sh, 1), 0.0)
        sh *= 2
    return x


def _cum_pages(lf, carry):
    n_rows = lf.shape[0]
    loc = _lane_prefix(lf)
    tot = jnp.broadcast_to(loc[:, LANES - 1:LANES], loc.shape)
    run = tot
    rowi = _iota(loc.shape, 0)
    sh = N_HEADS
    while sh < n_rows:
        run = run + jnp.where(rowi >= sh, pltpu.roll(run, sh, 0), 0.0)
        sh *= 2
    carry_t = carry if n_rows == N_HEADS else jnp.concatenate([carry] * (n_rows // N_HEADS), axis=0)
    return loc + (run - tot) + carry_t, run[n_rows - N_HEADS:, :] + carry


def _tile_rows(x, n):
    return jnp.concatenate([x] * n, axis=0)


def _tile_lanes(x, n):
    return jnp.concatenate([x] * n, axis=1)


def _fox_dec_kernel(pt_ref, q_ref, kn_ref, vn_ref, lfn_ref, *refs, n_pg):
    k_refs = refs[0:n_pg]
    v_refs = refs[n_pg:2 * n_pg]
    lf_refs = refs[2 * n_pg:3 * n_pg]
    o_ref, m_scr, l_scr, acc_scr, carry_scr = refs[3 * n_pg:]
    j = pl.program_id(1)
    n_tok = q_ref.shape[1]

    @pl.when(j == 0)
    def _():
        m_scr[...] = jnp.full_like(m_scr, NEG)
        l_scr[...] = jnp.zeros_like(l_scr)
        acc_scr[...] = jnp.zeros_like(acc_scr)
        carry_scr[...] = jnp.zeros_like(carry_scr)

    qbd = _qbd_rows(q_ref[0].astype(F32)).astype(BF16)

    def attend(s, pv):
        m_new, alpha, l_new, pe = _softmax_update(s, m_scr[...], l_scr[...])
        l_scr[...] = l_new
        acc_scr[...] = _tile_lanes(alpha, W_HEADS // LANES) * acc_scr[...] + pv(pe)
        m_scr[...] = m_new

    lf = jnp.concatenate([r[0, 0] for r in lf_refs], axis=0)
    cum, carry = _cum_pages(lf, carry_scr[...])
    carry_scr[...] = carry
    kt = _page_matrix(k_refs)
    vt = _page_matrix(v_refs)
    bias = jnp.concatenate([_tile_rows(cum[i * N_HEADS:(i + 1) * N_HEADS], n_tok)
                            for i in range(n_pg)], axis=1)
    attend(_dot(qbd, kt) - bias * LOG2E, lambda pe: _dot_nt(pe, vt))

    @pl.when(j == pl.num_programs(1) - 1)
    def _():
        cn, _ = _cum_pages(lfn_ref[0], carry_scr[...])
        sn = _dot_nt(qbd, kn_ref[0].astype(BF16)) - _tile_rows(cn, n_tok) * LOG2E
        ok = _iota(sn.shape, 1) <= _iota(sn.shape, 0) // N_HEADS
        vn = vn_ref[0].astype(BF16)
        attend(jnp.where(ok, sn, NEG), lambda pe: _dot(pe, vn))
        l_tot = jnp.sum(l_scr[...], axis=-1, keepdims=True)
        o_ref[0] = _rows_to_tokens(acc_scr[...] / l_tot, n_tok)


def _page_specs(n_pg, tail_shape, layer):
    def one(i):
        return pl.BlockSpec((1, 1) + tail_shape,
                            lambda bb, j, pt, i=i: (pt[bb, j * n_pg + i], layer) + (0,) * len(tail_shape))
    return [one(i) for i in range(n_pg)]


def _fox_decode(page_table, q, kn, vn, lfn, cache_kt, cache_vt, cache_lft, layer):
    bsz, n_pages = page_table.shape
    page = cache_kt.shape[-1]
    n_tok = q.shape[1]
    rows = n_tok * N_HEADS
    n_pg = next(n for n in (16, 8, 4, 2, 1) if n_pages % n == 0)
    assert page == LANES and n_tok <= page
    seq = lambda bb, j, pt: (bb, 0, 0)
    in_specs = [pl.BlockSpec((1, n_tok, W_HEADS), seq), pl.BlockSpec((1, page, W_HEADS), seq),
                pl.BlockSpec((1, page, W_HEADS), seq), pl.BlockSpec((1, N_HEADS, page), seq)]
    in_specs += _page_specs(n_pg, (N_HEADS, HEAD_DIM, page), layer) * 2
    in_specs += _page_specs(n_pg, (N_HEADS, page), layer)
    grid_spec = pltpu.PrefetchScalarGridSpec(
        num_scalar_prefetch=1, grid=(bsz, n_pages // n_pg), in_specs=in_specs,
        out_specs=pl.BlockSpec((1, n_tok, W_HEADS), seq),
        scratch_shapes=[pltpu.VMEM((rows, LANES), F32), pltpu.VMEM((rows, LANES), F32),
                        pltpu.VMEM((rows, W_HEADS), F32), pltpu.VMEM((N_HEADS, LANES), F32)])
    return pl.pallas_call(
        functools.partial(_fox_dec_kernel, n_pg=n_pg),
        grid_spec=grid_spec,
        out_shape=jax.ShapeDtypeStruct((bsz, n_tok, W_HEADS), F32),
        compiler_params=_cparams(("parallel", "arbitrary")),
    )(page_table, q, kn, vn, lfn, *([cache_kt] * n_pg), *([cache_vt] * n_pg), *([cache_lft] * n_pg))


def _moba_dec_kernel(pt_ref, q_ref, kn_ref, vn_ref, *refs, n_pg, n_blk, bps):
    k_refs = refs[0:n_pg]
    v_refs = refs[n_pg:2 * n_pg]
    o_ref, g_scr, m_scr, l_scr, o_scr = refs[2 * n_pg:]
    n = pl.program_id(1)
    n_tok = q_ref.shape[1]
    rows = n_tok * N_HEADS
    lane = _iota((1, LANES), 1)

    @pl.when(n == 0)
    def _():
        g_scr[...] = jnp.zeros_like(g_scr)
        m_scr[...] = jnp.full_like(m_scr, NEG)
        l_scr[...] = jnp.zeros_like(l_scr)

    qh, ql = _split2(_qbd_rows(q_ref[0] * Q_SCALE))
    kt = _page_matrix(k_refs)
    vt = _page_matrix(v_refs)
    s2 = _dot(jnp.concatenate([qh, ql], axis=0), kt)
    bw = kt.shape[1] // bps
    g_new, m_new, l_new = g_scr[...], m_scr[...], l_scr[...]
    for bb in range(bps):
        blk = n * bps + bb
        s = s2[0:rows, bb * bw:(bb + 1) * bw]
        gate = jnp.sum(s + s2[rows:2 * rows, bb * bw:(bb + 1) * bw], axis=-1, keepdims=True)
        ch = _lane_chunks(s)
        mx = ch[0]
        for c in ch[1:]:
            mx = jnp.maximum(mx, c)
        m_n = jnp.max(mx, axis=-1, keepdims=True)
        ps = [jnp.exp2(c - m_n) for c in ch]
        lsum = ps[0]
        for p_ in ps[1:]:
            lsum = lsum + p_
        onehot = lane == blk
        g_new = jnp.where(onehot, gate, g_new)
        m_new = jnp.where(onehot, m_n, m_new)
        l_new = jnp.where(onehot, jnp.sum(lsum, axis=-1, keepdims=True), l_new)
        o_scr[blk] = _dot_nt(jnp.concatenate([p_.astype(BF16) for p_ in ps], axis=1),
                             vt[:, bb * bw:(bb + 1) * bw])
    g_scr[...] = g_new
    m_scr[...] = m_new
    l_scr[...] = l_new

    @pl.when(n == pl.num_programs(1) - 1)
    def _():
        sel = _top3_select(jnp.where(lane < n_blk, g_scr[...], NEG),
                           lane + jnp.zeros((rows, LANES), jnp.int32))
        sn = _dot_nt(qh, kn_ref[0].astype(BF16))
        ok = _iota(sn.shape, 1) <= _iota(sn.shape, 0) // N_HEADS
        sn = jnp.where(ok, sn, NEG)
        m_o = jnp.max(sn, axis=-1, keepdims=True)
        pn = jnp.exp2(sn - m_o)
        l_o = jnp.sum(pn, axis=-1, keepdims=True)
        o_o = _dot(pn.astype(BF16), vn_ref[0].astype(BF16))
        m_all = m_scr[...]
        m_tot = jnp.maximum(m_o, jnp.max(jnp.where(sel, m_all, NEG), axis=-1, keepdims=True))
        w = jnp.where(sel, jnp.exp2(m_all - m_tot), 0.0)
        e_o = jnp.exp2(m_o - m_tot)
        l_tot = l_o * e_o + jnp.sum(w * l_scr[...], axis=-1, keepdims=True)
        out = o_o * e_o
        for b_i in range(n_blk):
            out = out + w[:, b_i:b_i + 1] * o_scr[b_i]
        o_ref[0] = _rows_to_tokens(out / l_tot, n_tok)


def _moba_decode(page_table, q, kn, vn, cache_kt, cache_vt, layer):
    bsz, n_pages = page_table.shape
    page = cache_kt.shape[-1]
    assert MOBA_BLOCK % page == 0 and page == LANES
    pages_per_blk = MOBA_BLOCK // page
    assert n_pages % pages_per_blk == 0
    n_blk = n_pages // pages_per_blk
    assert n_blk <= LANES
    bps = next(n for n in (8, 4, 2, 1) if n_blk % n == 0)
    n_pg = bps * pages_per_blk
    n_tok = q.shape[1]
    rows = n_tok * N_HEADS
    seq = lambda bb, j, pt: (bb, 0, 0)
    in_specs = [pl.BlockSpec((1, n_tok, W_HEADS), seq), pl.BlockSpec((1, page, W_HEADS), seq),
                pl.BlockSpec((1, page, W_HEADS), seq)]
    in_specs += _page_specs(n_pg, (N_HEADS, HEAD_DIM, page), layer) * 2
    grid_spec = pltpu.PrefetchScalarGridSpec(
        num_scalar_prefetch=1, grid=(bsz, n_blk // bps), in_specs=in_specs,
        out_specs=pl.BlockSpec((1, n_tok, W_HEADS), seq),
        scratch_shapes=[pltpu.VMEM((rows, LANES), F32), pltpu.VMEM((rows, LANES), F32),
                        pltpu.VMEM((rows, LANES), F32), pltpu.VMEM((n_blk, rows, W_HEADS), F32)])
    return pl.pallas_call(
        functools.partial(_moba_dec_kernel, n_pg=n_pg, n_blk=n_blk, bps=bps),
        grid_spec=grid_spec,
        out_shape=jax.ShapeDtypeStruct((bsz, n_tok, W_HEADS), F32),
        compiler_params=_cparams(("parallel", "arbitrary")),
    )(page_table, q, kn, vn, *([cache_kt] * n_pg), *([cache_vt] * n_pg))


def _rope_tables(pos):
    half = ROPE_DIM // 2
    inv = ROPE_THETA ** (-jnp.arange(half, dtype=F32) / half)
    ang = pos.astype(F32)[:, None] * inv[None, :]
    cos, sin = jnp.cos(ang), jnp.sin(ang)
    t = pos.shape[0]
    one = jnp.ones((t, HEAD_DIM - ROPE_DIM), F32)
    zero = jnp.zeros((t, HEAD_DIM - ROPE_DIM), F32)
    z8 = jnp.zeros((t, half), F32)
    cos_h = jnp.concatenate([cos, cos, one], axis=1)
    sp_h = jnp.concatenate([z8, sin, zero], axis=1)
    sm_h = jnp.concatenate([-sin, z8, zero], axis=1)
    tile2 = lambda a: jnp.concatenate([a, a], axis=1)
    return tile2(cos_h), tile2(sp_h), tile2(sm_h)


def _layer_params(l, p):
    d = p["w_in"].shape[1]
    w_in = p["w_in"][l]
    o = 0
    w_fox = w_in[:, o:o + 3 * W_HEADS]; o += 3 * W_HEADS
    w_ff = w_in[:, o:o + N_HEADS]; o += N_HEADS
    w_moba = w_in[:, o:o + 3 * W_HEADS]; o += 3 * W_HEADS
    w_ur = w_in[:, o:o + RWKV_COLS]; o += RWKV_COLS
    w_ug = w_in[:, o:o + N_BRANCH * d]
    pad_l = lambda a, n: jnp.pad(a, ((0, 0), (0, n - a.shape[1])))
    tile_h = lambda a: jnp.tile(a, N_HEADS)[None, :]
    zeros64 = jnp.zeros((64, W_HEADS), F32)
    return dict(
        norm_mix=p["norm_mix"][l][None, :],
        w_fox=w_fox.astype(BF16), w_ff=pad_l(w_ff, LANES).astype(BF16),
        b_ff=pad_l(p["b_forget"][l][None, :], LANES),
        w_moba=w_moba.astype(BF16), w_ur=w_ur.astype(BF16), w_ug=w_ug.astype(BF16),
        qn_fox=tile_h(p["qn_fox"][l]), kn_fox=tile_h(p["kn_fox"][l]),
        qn_moba=tile_h(p["qn_moba"][l]), kn_moba=tile_h(p["kn_moba"][l]),
        mu=p["rwkv_mu"][l][None, :], w0=p["rwkv_w0"][l][None, :], a0=p["rwkv_a0"][l][None, :],
        w_up=jnp.concatenate([p["rwkv_w_up"][l], zeros64], axis=0).astype(BF16),
        a_up=jnp.concatenate([zeros64, p["rwkv_a_up"][l]], axis=0).astype(BF16),
        g_up=p["rwkv_g_up"][l].astype(BF16),
        k_k=p["rwkv_k_k"][l][None, :], k_a=p["rwkv_k_a"][l][None, :],
        r_k=p["rwkv_r_k"][l].reshape(1, W_HEADS),
        lnx_w=p["rwkv_lnx_w"][l][None, :], lnx_b=p["rwkv_lnx_b"][l][None, :],
        w_br_fox=p["w_br_fox"][l].astype(BF16), w_br_rwkv=p["w_br_rwkv"][l].astype(BF16),
        w_br_moba=p["w_br_moba"][l].astype(BF16), w_out=p["w_out"][l].astype(BF16),
        norm_mlp=p["norm_mlp"][l][None, :],
        w_up_mlp=p["w_mlp_up"][l].astype(BF16), w_down_mlp=p["w_mlp_down"][l].astype(BF16),
    )


def _state_to_blockdiag(s):
    b = s.shape[0]
    st = jnp.swapaxes(s, -1, -2).reshape(b, N_PAIRS, 2, HEAD_DIM, HEAD_DIM)
    z = jnp.zeros_like(st[:, :, 0])
    top = jnp.concatenate([st[:, :, 0], z], axis=-1)
    bot = jnp.concatenate([z, st[:, :, 1]], axis=-1)
    return jnp.concatenate([top, bot], axis=-2)


def _blockdiag_to_state(sbd):
    b = sbd.shape[0]
    s0 = sbd[:, :, 0:HEAD_DIM, 0:HEAD_DIM]
    s1 = sbd[:, :, HEAD_DIM:, HEAD_DIM:]
    st = jnp.stack([s0, s1], axis=2).reshape(b, N_HEADS, HEAD_DIM, HEAD_DIM)
    return jnp.swapaxes(st, -1, -2)


def _mixers_common(x2, lp, gs_ones, place, rope, shift0, s0, bsz, t, tiles_per_seq,
                   fox_stack=None, moba_stack=None):
    m = x2.shape[0]
    qs, qe, qo, ke, ko, fk, fv, fvb, logf = _proj_fox(
        x2, lp["norm_mix"], lp["w_fox"], lp["w_ff"], lp["b_ff"], lp["qn_fox"], lp["kn_fox"],
        gs_ones, place, tiles_per_seq, stack=fox_stack)
    mq, mk, mv, mvb = _proj_moba(x2, lp["norm_mix"], lp["w_moba"], lp["qn_moba"], lp["kn_moba"],
                                 gs_ones, *rope, tiles_per_seq, stack=moba_stack)
    ur = _proj_plain(x2, lp["norm_mix"], lp["w_ur"])
    gates = _proj_plain(x2, lp["norm_mix"], lp["w_ug"], act="sigmoid")
    ur3 = ur.reshape(bsz, t, RWKV_COLS)
    r, ld, k2, v, kk, b, bonus, g = _rwkv_prep(
        ur, shift0, t, lp["mu"], lp["w0"], lp["w_up"], lp["a0"], lp["a_up"], lp["g_up"],
        lp["k_k"], lp["k_a"], lp["r_k"], gs_ones)
    t_pad = -(-t // RWKV_CHUNK) * RWKV_CHUNK
    seqs = [a.reshape(bsz, t, W_HEADS) for a in (r, ld, k2, v, kk, b)]
    if t_pad != t:
        seqs = [jnp.pad(a, ((0, 0), (0, t_pad - t), (0, 0))) for a in seqs]
    y_r, s_bd = _rwkv_scan(*seqs, _state_to_blockdiag(s0), nb=2 if bsz % 2 == 0 else 1)
    y_r = y_r[:, :t].reshape(m, W_HEADS)
    return dict(qs=qs, qe=qe, qo=qo, ke=ke, ko=ko, fk=fk, fv=fv, fvb=fvb, logf=logf,
                mq=mq, mk=mk, mv=mv, mvb=mvb, gates=gates,
                y_r=y_r, bonus=bonus, g=g, s_new=_blockdiag_to_state(s_bd), shift_new=ur3[:, -1])


def _finish_layer(x2, lp, gs_ones, pc, y_fox, y_moba):
    xo = _merge(x2, y_fox, y_moba, pc["y_r"], pc["bonus"], pc["g"], pc["gates"],
                lp["lnx_w"], lp["lnx_b"], gs_ones, lp["w_br_fox"], lp["w_br_rwkv"],
                lp["w_br_moba"], lp["w_out"])
    return _mlp(xo, lp["norm_mlp"], lp["w_up_mlp"], lp["w_down_mlp"])


def _flash_tile(t):
    for tile in (1024, 512, MOBA_BLOCK):
        if t % tile == 0:
            return tile
    raise ValueError("prompt length must be a multiple of the MoBA block")


def kernel(x_prompt, x_sample, cache_fox_k, cache_fox_v, cache_fox_logf, cache_moba_k, cache_moba_v, state_rwkv, state_rwkv_shift, page_table, norm_mix, w_in, b_forget, qn_fox, kn_fox, qn_moba, kn_moba, rwkv_mu, rwkv_w0, rwkv_w_up, rwkv_a0, rwkv_a_up, rwkv_g_up, rwkv_k_k, rwkv_k_a, rwkv_r_k, rwkv_lnx_w, rwkv_lnx_b, w_br_fox, w_br_rwkv, w_br_moba, w_out, norm_mlp, w_mlp_up, w_mlp_down):
    params = dict(norm_mix=norm_mix, w_in=w_in, b_forget=b_forget, qn_fox=qn_fox, kn_fox=kn_fox,
                  qn_moba=qn_moba, kn_moba=kn_moba, rwkv_mu=rwkv_mu, rwkv_w0=rwkv_w0,
                  rwkv_w_up=rwkv_w_up, rwkv_a0=rwkv_a0, rwkv_a_up=rwkv_a_up, rwkv_g_up=rwkv_g_up,
                  rwkv_k_k=rwkv_k_k, rwkv_k_a=rwkv_k_a, rwkv_r_k=rwkv_r_k, rwkv_lnx_w=rwkv_lnx_w,
                  rwkv_lnx_b=rwkv_lnx_b, w_br_fox=w_br_fox, w_br_rwkv=w_br_rwkv,
                  w_br_moba=w_br_moba, w_out=w_out, norm_mlp=norm_mlp, w_mlp_up=w_mlp_up,
                  w_mlp_down=w_mlp_down)
    depth = w_in.shape[0]
    bp, tp, d = x_prompt.shape
    bs, ts, _ = x_sample.shape
    n_pages = page_table.shape[1]
    page = cache_fox_k.shape[2]
    past_len = n_pages * page
    assert past_len % MOBA_BLOCK == 0 and ts <= MOBA_BLOCK and tp % MOBA_BLOCK == 0
    tile = _flash_tile(tp)
    tm = 256
    assert (bp * tp) % tm == 0 and tp % tm == 0 and (bs * ts) % tm == 0

    hid = jnp.arange(W_HEADS) // HEAD_DIM
    gs_ones = (hid[:, None] == hid[None, :]).astype(BF16)
    place = _bias_placement()
    rope_p = _rope_tables(jnp.arange(tp, dtype=jnp.int32))
    rope_s = tuple(jnp.tile(a, (bs, 1)) for a in _rope_tables(past_len + jnp.arange(ts, dtype=jnp.int32)))
    tview = lambda c: jnp.transpose(c, (0, 1, 3, 4, 2))
    fox_kt, fox_vt, moba_kt, moba_vt = (tview(c) for c in (cache_fox_k, cache_fox_v, cache_moba_k, cache_moba_v))
    fox_lft = jnp.transpose(cache_fox_logf, (0, 1, 3, 2))

    xp = x_prompt.reshape(bp * tp, d)
    xs = x_sample.reshape(bs * ts, d)
    rows_p = [[] for _ in range(3)]
    rows_s = [[] for _ in range(7)]
    n_stack = bp * depth * tp
    fk_all, fv_all, mk_all, mv_all = (jnp.zeros((n_stack, W_HEADS), F32) for _ in range(4))
    for l in range(depth):
        lp = _layer_params(l, params)

        pc = _mixers_common(xp, lp, gs_ones, place, rope_p,
                            jnp.zeros((bp, RWKV_COLS), F32),
                            jnp.zeros((bp, N_HEADS, HEAD_DIM, HEAD_DIM), F32), bp, tp, tp // tm,
                            fox_stack=(fk_all, fv_all, l, depth, n_stack),
                            moba_stack=(mk_all, mv_all, l, depth, n_stack))
        fk_all, fv_all, mk_all, mv_all = pc["fk"], pc["fv"], pc["mk"], pc["mv"]
        b3 = lambda a: a.reshape(bp, tp, W_HEADS)
        y_fox = _flash(b3(pc["qe"]), b3(pc["qo"]), b3(pc["ke"]), b3(pc["ko"]), b3(pc["fvb"]), tile)
        mqe, mqo, mke, mko = _moba_gate(b3(pc["mq"]), mk_all.reshape(bp, depth * tp, W_HEADS),
                                        l * (tp // MOBA_BLOCK))
        y_moba = _flash(mqe, mqo, mke, mko, b3(pc["mvb"]), tile)
        xp = _finish_layer(xp, lp, gs_ones, pc, y_fox.reshape(bp * tp, W_HEADS),
                           y_moba.reshape(bp * tp, W_HEADS))
        for acc, val in zip(rows_p, (pc["logf"].reshape(bp, tp, N_HEADS), pc["s_new"], pc["shift_new"])):
            acc.append(val)

        sc = _mixers_common(xs, lp, gs_ones, place, rope_s, state_rwkv_shift[:, l],
                            state_rwkv[:, l], bs, ts, 1)
        s3 = lambda a: a.reshape(bs, ts, W_HEADS)
        padt = lambda a: jnp.pad(s3(a), ((0, 0), (0, page - ts), (0, 0)))
        lfn = jnp.pad(jnp.swapaxes(sc["logf"].reshape(bs, ts, N_HEADS), 1, 2),
                      ((0, 0), (0, 0), (0, page - ts)))
        y_fox_s = _fox_decode(page_table, s3(sc["qs"]), padt(sc["fk"]), padt(sc["fv"]), lfn,
                              fox_kt, fox_vt, fox_lft, l)
        y_moba_s = _moba_decode(page_table, s3(sc["mq"]), padt(sc["mk"]), padt(sc["mv"]),
                                moba_kt, moba_vt, l)
        xs = _finish_layer(xs, lp, gs_ones, sc, y_fox_s.reshape(bs * ts, W_HEADS),
                           y_moba_s.reshape(bs * ts, W_HEADS))
        h5s = lambda a: a.reshape(bs, ts, N_HEADS, HEAD_DIM)
        for acc, val in zip(rows_s, (h5s(sc["fk"]), h5s(sc["fv"]), sc["logf"].reshape(bs, ts, N_HEADS),
                                     h5s(sc["mk"]), h5s(sc["mv"]), sc["s_new"], sc["shift_new"])):
            acc.append(val)

    p_logf, p_rwkv, p_shift = (jnp.stack(a, axis=1) for a in rows_p)
    kv5 = lambda a: a.reshape(bp, depth, tp, N_HEADS, HEAD_DIM)
    outs_s = [jnp.stack(a, axis=1) for a in rows_s]
    return (xp.reshape(bp, tp, d), xs.reshape(bs, ts, d), kv5(fk_all), kv5(fv_all), p_logf,
            kv5(mk_all), kv5(mv_all), p_rwkv, p_shift, *outs_s)
```

```python
import functools
import math

import numpy as np
import jax
import jax.numpy as jnp
from jax import lax
from jax.experimental import pallas as pl
from jax.experimental.pallas import tpu as pltpu

F32 = jnp.float32
BF16 = jnp.bfloat16

HEAD_DIM = 64
N_HEADS = 8
W_HEADS = N_HEADS * HEAD_DIM
N_PAIRS = N_HEADS // 2
RWKV_COLS = 3 * W_HEADS + 64 + 64 + 128
N_BRANCH = 3
MOBA_BLOCK = 256
MOBA_TOPK = 3
ROPE_THETA = 500000.0
ROPE_DIM = HEAD_DIM // 4
NORM_EPS = 1e-6
LNX_EPS = 64e-5
L2_EPS = 1e-12
LOG2E = 1.4426950408889634
Q_SCALE = HEAD_DIM ** -0.5 * LOG2E

LANES = 128
SUBLANES = 8
VMEM_LIMIT = 56 * 1024 * 1024

NEG = -1e30
BIAS_OFF = -(2.0 ** 100)
N_BIAS_LANES = 3
RWKV_CHUNK = 64


def _cparams(sem):
    return pltpu.CompilerParams(dimension_semantics=sem, vmem_limit_bytes=VMEM_LIMIT)


def _dot(a, b):
    return jnp.dot(a, b, preferred_element_type=F32)


def _dot_nt(a, b):
    return lax.dot_general(a, b, (((1,), (1,)), ((), ())), preferred_element_type=F32)


def _split2(x):
    hi = x.astype(BF16)
    lo = (x - hi.astype(F32)).astype(BF16)
    return hi, lo


def _split3(x):
    hi = x.astype(BF16)
    r1 = x - hi.astype(F32)
    mid = r1.astype(BF16)
    lo = (r1 - mid.astype(F32)).astype(BF16)
    return hi, mid, lo


def _dot1(a, b, dot=_dot):
    return dot(a.astype(BF16), b.astype(BF16))


def _dot3(a, b, dot=_dot):
    ah, al = _split2(a)
    bh, bl = _split2(b)
    return dot(ah, bh) + (dot(ah, bl) + dot(al, bh))


def _dot_exact_lhs(a_bf16, b, dot=_dot):
    h, m, l = _split3(b)
    return dot(a_bf16, h) + (dot(a_bf16, m) + dot(a_bf16, l))


def _head_reduce(x, g_bf16):
    hi, lo = _split2(x)
    return _dot(hi, g_bf16) + _dot(lo, g_bf16)


def _softplus(z):
    return jnp.maximum(z, 0.0) + jnp.log(1.0 + jnp.exp(-jnp.abs(z)))


def _sigmoid(z):
    return 1.0 / (1.0 + jnp.exp(-z))


def _rms(x, g):
    ms = jnp.mean(x * x, axis=-1, keepdims=True)
    return x * lax.rsqrt(ms + NORM_EPS) * g


def _iota(shape, dim):
    return lax.broadcasted_iota(jnp.int32, shape, dim)


def _lane_chunks(s):
    return [s[:, j * LANES:(j + 1) * LANES] for j in range(s.shape[1] // LANES)]


def _softmax_update(s, m_prev, l_prev):
    ch = _lane_chunks(s)
    mx = ch[0]
    for c in ch[1:]:
        mx = jnp.maximum(mx, c)
    m_new = jnp.maximum(m_prev, jnp.max(mx, axis=-1, keepdims=True))
    alpha = jnp.exp2(m_prev - m_new)
    ps = [jnp.exp2(c - m_new) for c in ch]
    lsum = ps[0]
    for p_ in ps[1:]:
        lsum = lsum + p_
    pe = ps[0].astype(BF16) if len(ps) == 1 else jnp.concatenate([p_.astype(BF16) for p_ in ps], axis=1)
    return m_new, alpha, alpha * l_prev + lsum, pe


def _proj_plain_kernel(x_ref, g_ref, w_ref, o_ref, *, act):
    h = _rms(x_ref[...], g_ref[...]).astype(BF16)
    u = _dot(h, w_ref[...])
    if act == "sigmoid":
        u = _sigmoid(u)
    o_ref[...] = u


def _proj_plain(x, g, w, act=None, tm=256):
    m, d = x.shape
    n = w.shape[1]
    return pl.pallas_call(
        functools.partial(_proj_plain_kernel, act=act),
        grid=(m // tm,),
        in_specs=[pl.BlockSpec((tm, d), lambda i: (i, 0)),
                  pl.BlockSpec((1, d), lambda i: (0, 0)),
                  pl.BlockSpec((d, n), lambda i: (0, 0))],
        out_specs=pl.BlockSpec((tm, n), lambda i: (i, 0)),
        out_shape=jax.ShapeDtypeStruct((m, n), F32),
        compiler_params=_cparams(("parallel",)),
    )(x, g, w)


def _proj_fox_kernel(x_ref, g_ref, w_ref, wf_ref, bf_ref, qn_ref, kn_ref, gm_ref, place_ref, *rest,
                     tiles_per_seq, n_alias):
    qs_ref, qe_ref, qo_ref, ke_ref, ko_ref, k_ref, v_ref, vb_ref, lf_ref, carry_scr = rest[n_alias:]
    i = pl.program_id(0)
    tm = x_ref.shape[0]
    h = _rms(x_ref[...], g_ref[...]).astype(BF16)
    u = _dot(h, w_ref[...])
    gm = gm_ref[...]
    fq = u[:, 0:W_HEADS]
    fk = u[:, W_HEADS:2 * W_HEADS]
    q_ms = _head_reduce(fq * fq, gm) * (1.0 / HEAD_DIM)
    k_ms = _head_reduce(fk * fk, gm) * (1.0 / HEAD_DIM)
    q = fq * lax.rsqrt(q_ms + NORM_EPS) * qn_ref[...] * Q_SCALE
    k = fk * lax.rsqrt(k_ms + NORM_EPS) * kn_ref[...]
    v = u[:, 2 * W_HEADS:3 * W_HEADS]
    k_ref[...] = k
    v_ref[...] = v
    vb_ref[...] = v.astype(BF16)
    z = _dot(h, wf_ref[...]) + bf_ref[...]
    lane = _iota((1, LANES), 1)
    lf = jnp.where(lane < N_HEADS, -_softplus(-z), 0.0)
    lf_ref[...] = lf[:, 0:N_HEADS]

    @pl.when(i % tiles_per_seq == 0)
    def _():
        carry_scr[...] = jnp.zeros_like(carry_scr)

    tri = (_iota((tm, tm), 0) >= _iota((tm, tm), 1)).astype(BF16)
    c = _dot_exact_lhs(tri, lf) + carry_scr[...]
    carry_scr[...] = c[tm - 1:tm, :]

    lane512 = _iota((1, W_HEADS), 1)
    even = (lane512 // HEAD_DIM) % 2 == 0
    lt = lane512 % LANES
    ones_e = jnp.where(jnp.logical_and(lt >= HEAD_DIM, lt < HEAD_DIM + N_BIAS_LANES), 1.0, 0.0)
    ones_o = jnp.where(lt < N_BIAS_LANES, 1.0, 0.0)
    qs_ref[...] = q.astype(BF16)
    qe_ref[...] = jnp.where(even, q, ones_e).astype(BF16)
    qo_ref[...] = jnp.where(even, ones_o, q).astype(BF16)
    pieces = _split3(c * LOG2E)
    kb_e = _dot(pieces[0], place_ref[0]) + _dot(pieces[1], place_ref[1]) + _dot(pieces[2], place_ref[2])
    kb_o = _dot(pieces[0], place_ref[3]) + _dot(pieces[1], place_ref[4]) + _dot(pieces[2], place_ref[5])
    ke_ref[...] = (jnp.where(even, k, 0.0) + kb_e).astype(BF16)
    ko_ref[...] = (jnp.where(even, 0.0, k) + kb_o).astype(BF16)


def _bias_placement():
    pm = np.zeros((2 * N_BIAS_LANES, LANES, W_HEADS), np.float32)
    for h in range(N_HEADS):
        p = h // 2
        for j in range(N_BIAS_LANES):
            if h % 2 == 0:
                pm[j, h, p * LANES + HEAD_DIM + j] = -1.0
            else:
                pm[N_BIAS_LANES + j, h, p * LANES + j] = -1.0
    return jnp.asarray(pm, BF16)


def _stack_rows(stack, tiles_per_seq, tm):
    k_st, v_st, layer, depth, n_rows = stack
    srow = lambda i: (((i // tiles_per_seq) * depth + layer) * tiles_per_seq + i % tiles_per_seq, 0)
    return (pl.BlockSpec((tm, W_HEADS), srow), jax.ShapeDtypeStruct((n_rows, W_HEADS), F32),
            [k_st, v_st], [pl.BlockSpec(memory_space=pl.ANY)] * 2)


def _proj_fox(x, g, w, wf, bfg, qn, kn, gm, place, tiles_per_seq, tm=256, stack=None):
    m, d = x.shape
    row = lambda i: (i, 0)
    fix = lambda i: (0, 0)
    act = pl.BlockSpec((tm, W_HEADS), row)
    bf_out = jax.ShapeDtypeStruct((m, W_HEADS), BF16)
    kv_spec, kv_shape, extra, extra_specs, aliases = act, jax.ShapeDtypeStruct((m, W_HEADS), F32), [], [], {}
    if stack is not None:
        kv_spec, kv_shape, extra, extra_specs = _stack_rows(stack, tiles_per_seq, tm)
        aliases = {9: 5, 10: 6} if extra else {}
    return pl.pallas_call(
        functools.partial(_proj_fox_kernel, tiles_per_seq=tiles_per_seq, n_alias=len(extra)),
        grid=(m // tm,),
        in_specs=[pl.BlockSpec((tm, d), row), pl.BlockSpec((1, d), fix),
                  pl.BlockSpec((d, 3 * W_HEADS), fix), pl.BlockSpec((d, LANES), fix),
                  pl.BlockSpec((1, LANES), fix), pl.BlockSpec((1, W_HEADS), fix),
                  pl.BlockSpec((1, W_HEADS), fix), pl.BlockSpec((W_HEADS, W_HEADS), fix),
                  pl.BlockSpec((2 * N_BIAS_LANES, LANES, W_HEADS), lambda i: (0, 0, 0))] + extra_specs,
        out_specs=[act] * 5 + [kv_spec, kv_spec, act, pl.BlockSpec((tm, N_HEADS), row)],
        out_shape=[bf_out] * 5 + [kv_shape, kv_shape, bf_out, jax.ShapeDtypeStruct((m, N_HEADS), F32)],
        scratch_shapes=[pltpu.VMEM((1, LANES), F32)],
        input_output_aliases=aliases,
        compiler_params=_cparams(("arbitrary",)),
    )(x, g, w, wf, bfg, qn, kn, gm, place, *extra)


def _rotary(x, cos, sp, sm):
    outs = []
    for j in range(W_HEADS // LANES):
        xs = x[:, j * LANES:(j + 1) * LANES]
        outs.append(xs * cos + pltpu.roll(xs, ROPE_DIM // 2, 1) * sp
                    + pltpu.roll(xs, LANES - ROPE_DIM // 2, 1) * sm)
    return jnp.concatenate(outs, axis=1)


def _proj_moba_kernel(x_ref, g_ref, w_ref, qn_ref, kn_ref, gm_ref, cos_ref, sp_ref, sm_ref, *rest, n_alias):
    q_ref, k_ref, v_ref, vb_ref = rest[n_alias:]
    h = _rms(x_ref[...], g_ref[...]).astype(BF16)
    u = _dot(h, w_ref[...])
    gm = gm_ref[...]
    mq = u[:, 0:W_HEADS]
    mk = u[:, W_HEADS:2 * W_HEADS]
    q_ms = _head_reduce(mq * mq, gm) * (1.0 / HEAD_DIM)
    k_ms = _head_reduce(mk * mk, gm) * (1.0 / HEAD_DIM)
    cos, sp, sm = cos_ref[...], sp_ref[...], sm_ref[...]
    q_ref[...] = _rotary(mq * lax.rsqrt(q_ms + NORM_EPS) * qn_ref[...], cos, sp, sm)
    k_ref[...] = _rotary(mk * lax.rsqrt(k_ms + NORM_EPS) * kn_ref[...], cos, sp, sm)
    v = u[:, 2 * W_HEADS:3 * W_HEADS]
    v_ref[...] = v
    vb_ref[...] = v.astype(BF16)


def _proj_moba(x, g, w, qn, kn, gm, cos, sp, sm, tiles_per_seq, tm=256, stack=None):
    m, d = x.shape
    ntab = cos.shape[0] // tm
    row = lambda i: (i, 0)
    fix = lambda i: (0, 0)
    tab = lambda i: (i % ntab, 0)
    act = pl.BlockSpec((tm, W_HEADS), row)
    f_out = jax.ShapeDtypeStruct((m, W_HEADS), F32)
    kv_spec, kv_shape, extra, extra_specs, aliases = act, f_out, [], [], {}
    if stack is not None:
        kv_spec, kv_shape, extra, extra_specs = _stack_rows(stack, tiles_per_seq, tm)
        aliases = {9: 1, 10: 2} if extra else {}
    return pl.pallas_call(
        functools.partial(_proj_moba_kernel, n_alias=len(extra)),
        grid=(m // tm,),
        in_specs=[pl.BlockSpec((tm, d), row), pl.BlockSpec((1, d), fix),
                  pl.BlockSpec((d, 3 * W_HEADS), fix), pl.BlockSpec((1, W_HEADS), fix),
                  pl.BlockSpec((1, W_HEADS), fix), pl.BlockSpec((W_HEADS, W_HEADS), fix),
                  pl.BlockSpec((tm, LANES), tab), pl.BlockSpec((tm, LANES), tab),
                  pl.BlockSpec((tm, LANES), tab)] + extra_specs,
        out_specs=[act, kv_spec, kv_spec, act],
        out_shape=[f_out, kv_shape, kv_shape, jax.ShapeDtypeStruct((m, W_HEADS), BF16)],
        input_output_aliases=aliases,
        compiler_params=_cparams(("parallel",)),
    )(x, g, w, qn, kn, gm, cos, sp, sm, *extra)


def _flash_kernel(qt_ref, kt_ref, qe_ref, qo_ref, ke_ref, ko_ref, v_ref, o_ref, m_scr, l_scr, acc_scr,
                  *, tq, tk):
    s_idx = pl.program_id(2)
    qi = qt_ref[s_idx]
    ki = kt_ref[s_idx]

    @pl.when(ki == 0)
    def _():
        m_scr[...] = jnp.full_like(m_scr, NEG)
        l_scr[...] = jnp.zeros_like(l_scr)
        acc_scr[...] = jnp.zeros_like(acc_scr)

    pairs = ((qe_ref, ke_ref), (qo_ref, ko_ref))

    def update(h, rows, s, vv):
        m_new, alpha, l_new, pe = _softmax_update(s, m_scr[h, rows], l_scr[h, rows])
        l_scr[h, rows] = l_new
        acc_scr[h, rows] = alpha * acc_scr[h, rows] + _dot(pe, vv)
        m_scr[h, rows] = m_new

    @pl.when(ki < qi)
    def _():
        v = v_ref[0]
        for h, (q_ref, k_ref) in enumerate(pairs):
            update(h, slice(0, tq), _dot_nt(q_ref[0], k_ref[0]), v)

    @pl.when(ki == qi)
    def _():
        hq = tq // 2
        tri = _iota((hq, hq), 1) <= _iota((hq, hq), 0)
        v = v_ref[0]
        for h, (q_ref, k_ref) in enumerate(pairs):
            q, k = q_ref[0], k_ref[0]
            s_a = _dot_nt(q, k[0:hq])
            s_a = jnp.concatenate([jnp.where(tri, s_a[0:hq], NEG), s_a[hq:tq]], axis=0)
            update(h, slice(0, tq), s_a, v[0:hq])
            s_b = jnp.where(tri, _dot_nt(q[hq:tq], k[hq:tq]), NEG)
            update(h, slice(hq, tq), s_b, v[hq:tq])
        lane = _iota((1, LANES), 1)
        l0 = jnp.sum(l_scr[0], axis=-1, keepdims=True)
        l1 = jnp.sum(l_scr[1], axis=-1, keepdims=True)
        o_ref[0] = jnp.where(lane < HEAD_DIM, acc_scr[0] / l0, acc_scr[1] / l1)


def _tri_tables(nq):
    qt = np.concatenate([np.full(i + 1, i, np.int32) for i in range(nq)])
    kt = np.concatenate([np.arange(i + 1, dtype=np.int32) for i in range(nq)])
    return jnp.asarray(qt), jnp.asarray(kt)


def _flash(qe, qo, ke, ko, v, tile):
    b, t, _ = v.shape
    tq = tk = tile
    qt, kt = _tri_tables(t // tq)
    qmap = lambda bb, p, s, qt_, kt_: (bb, qt_[s], p)
    kmap = lambda bb, p, s, qt_, kt_: (bb, kt_[s], p)
    grid_spec = pltpu.PrefetchScalarGridSpec(
        num_scalar_prefetch=2, grid=(b, N_PAIRS, int(qt.shape[0])),
        in_specs=[pl.BlockSpec((1, tq, LANES), qmap)] * 2 + [pl.BlockSpec((1, tk, LANES), kmap)] * 3,
        out_specs=pl.BlockSpec((1, tq, LANES), qmap),
        scratch_shapes=[pltpu.VMEM((2, tq, LANES), F32)] * 3)
    return pl.pallas_call(
        functools.partial(_flash_kernel, tq=tq, tk=tk),
        grid_spec=grid_spec,
        out_shape=jax.ShapeDtypeStruct((b, t, W_HEADS), F32),
        compiler_params=_cparams(("parallel", "parallel", "arbitrary")),
    )(qt, kt, qe, qo, ke, ko, v)


def _top3_select(gm, n_idx, axis=-1):
    sel = jnp.zeros(gm.shape, jnp.bool_)
    big = jnp.int32(1 << 20)
    for _ in range(MOBA_TOPK):
        mx = jnp.max(gm, axis=axis, keepdims=True)
        is_max = jnp.logical_and(gm == mx, mx > 0.5 * NEG)
        idx = jnp.min(jnp.where(is_max, n_idx, big), axis=axis, keepdims=True)
        pick = n_idx == idx
        sel = jnp.logical_or(sel, pick)
        gm = jnp.where(pick, NEG, gm)
    return sel


def _moba_gate_kernel(q_ref, k_ref, qe_ref, qo_ref, ke_ref, ko_ref, kme_scr, kmo_scr, *, n_blk):
    i = pl.program_id(1)
    rows = q_ref.shape[1]

    @pl.when(i == 0)
    def _():
        kme_scr[...] = jnp.zeros_like(kme_scr)
        kmo_scr[...] = jnp.zeros_like(kmo_scr)

    q = q_ref[0]
    k = k_ref[0]
    lane512 = _iota((1, W_HEADS), 1)
    lane = _iota((1, LANES), 1)
    for p in range(N_PAIRS):
        sl = slice(p * LANES, (p + 1) * LANES)
        qpair = q[:, sl] * Q_SCALE
        kpair = k[:, sl]
        for h in range(2):
            head = 2 * p + h
            qm = jnp.where(lane512 // HEAD_DIM == head, q, 0.0)
            km = kme_scr[...] if h == 0 else kmo_scr[...]
            g_t = _dot3(km, qm, _dot_nt)
            base = HEAD_DIM if h == 0 else 0
            slot = _iota((LANES, 1), 0) - base
            valid_t = jnp.logical_and(slot >= 0, slot < i)
            sel_t = _top3_select(jnp.where(valid_t, g_t, NEG), slot, axis=0)
            keep_t = jnp.logical_or(jnp.logical_or(sel_t, slot == i), slot >= n_blk)
            bias = jnp.transpose(jnp.where(keep_t, 0.0, BIAS_OFF))
            n_idx = lane - base
            in_half = jnp.logical_and(n_idx >= 0, n_idx < HEAD_DIM)
            qa = jnp.where(in_half, bias, qpair).astype(BF16)
            ka = jnp.where(in_half, jnp.where(n_idx == i, 1.0, 0.0), kpair).astype(BF16)
            if h == 0:
                qe_ref[0, :, sl] = qa
                ke_ref[0, :, sl] = ka
            else:
                qo_ref[0, :, sl] = qa
                ko_ref[0, :, sl] = ka

    kmean = jnp.sum(k, axis=0, keepdims=True) * (1.0 / rows)
    kme_scr[pl.ds(HEAD_DIM + i, 1), :] = kmean
    kmo_scr[pl.ds(i, 1), :] = kmean


def _moba_gate(q, k, k_blk_off=0):
    b, t, _ = q.shape
    n_blk = t // MOBA_BLOCK
    assert n_blk <= HEAD_DIM
    blk = lambda bb, i: (bb, i, 0)
    return pl.pallas_call(
        functools.partial(_moba_gate_kernel, n_blk=n_blk),
        grid=(b, n_blk),
        in_specs=[pl.BlockSpec((1, MOBA_BLOCK, W_HEADS), blk),
                  pl.BlockSpec((1, MOBA_BLOCK, W_HEADS), lambda bb, i: (bb, k_blk_off + i, 0))],
        out_specs=[pl.BlockSpec((1, MOBA_BLOCK, W_HEADS), blk)] * 4,
        out_shape=[jax.ShapeDtypeStruct((b, t, W_HEADS), BF16)] * 4,
        scratch_shapes=[pltpu.VMEM((LANES, W_HEADS), F32)] * 2,
        compiler_params=_cparams(("parallel", "arbitrary")),
    )(q, k)


def _rwkv_prep_kernel(u_ref, up_ref, sh_ref, mu_ref, w0_ref, wup_ref, a0_ref, aup_ref, gup_ref,
                      kk_ref_w, ka_ref, rk_ref, gs_ref,
                      r_o, ld_o, k_o, v_o, kk_o, b_o, bonus_o, g_o, *, tiles_per_seq):
    u = u_ref[...]
    if tiles_per_seq is None:
        u_prev = up_ref[...]
    else:
        first = pl.program_id(0) % tiles_per_seq == 0
        row0 = jnp.where(first, sh_ref[0], up_ref[SUBLANES - 1:SUBLANES, :])
        u_prev = jnp.where(_iota(u.shape, 0) == 0, row0, pltpu.roll(u, 1, 0))
    um = u + (u_prev - u) * mu_ref[...]
    r = um[:, 0:W_HEADS]
    k = um[:, W_HEADS:2 * W_HEADS]
    v = um[:, 2 * W_HEADS:3 * W_HEADS]
    wa = um[:, 3 * W_HEADS:3 * W_HEADS + LANES]
    gd = um[:, 3 * W_HEADS + LANES:3 * W_HEADS + 2 * LANES]
    w_log = -_softplus(-(w0_ref[...] + _dot(jnp.tanh(wa).astype(BF16), wup_ref[...]))) - 0.5
    ld_o[...] = -jnp.exp(w_log)
    a = _sigmoid(a0_ref[...] + _dot(wa.astype(BF16), aup_ref[...]))
    g_o[...] = _dot(_sigmoid(gd).astype(BF16), gup_ref[...])
    gs = gs_ref[...]
    kx = k * kk_ref_w[...]
    kk = kx * lax.rsqrt(_head_reduce(kx * kx, gs) + L2_EPS)
    k2 = k * (1.0 + (a - 1.0) * ka_ref[...])
    r_o[...] = r
    k_o[...] = k2
    v_o[...] = v
    kk_o[...] = kk
    b_o[...] = kk * a
    bonus_o[...] = _head_reduce(r * k2 * rk_ref[...], gs) * v


def _rwkv_prep(u, shift0, t, mu, w0, wup, a0, aup, gup, k_k, k_a, r_k, gs, tm=256):
    m = u.shape[0]
    bsz = m // t
    row = lambda i: (i, 0)
    fix = lambda i: (0, 0)
    vec = pl.BlockSpec((1, W_HEADS), fix)
    lr = pl.BlockSpec((LANES, W_HEADS), fix)
    if t % tm == 0:
        tiles_per_seq = t // tm
        up = u
        up_spec = pl.BlockSpec((SUBLANES, RWKV_COLS),
                               lambda i: (jnp.maximum(i * (tm // SUBLANES) - 1, 0), 0))
        sh = shift0[:, None, :]
        sh_spec = pl.BlockSpec((1, 1, RWKV_COLS), lambda i: (i // tiles_per_seq, 0, 0))
    else:
        tiles_per_seq = None
        u3 = u.reshape(bsz, t, RWKV_COLS)
        up = jnp.concatenate([shift0[:, None, :], u3[:, :-1]], axis=1).reshape(m, RWKV_COLS)
        up_spec = pl.BlockSpec((tm, RWKV_COLS), row)
        sh = shift0[0:1, None, :]
        sh_spec = pl.BlockSpec((1, 1, RWKV_COLS), lambda i: (0, 0, 0))
    return pl.pallas_call(
        functools.partial(_rwkv_prep_kernel, tiles_per_seq=tiles_per_seq),
        grid=(m // tm,),
        in_specs=[pl.BlockSpec((tm, RWKV_COLS), row), up_spec, sh_spec,
                  pl.BlockSpec((1, RWKV_COLS), fix), vec, lr, vec, lr, lr, vec, vec, vec,
                  pl.BlockSpec((W_HEADS, W_HEADS), fix)],
        out_specs=[pl.BlockSpec((tm, W_HEADS), row)] * 8,
        out_shape=[jax.ShapeDtypeStruct((m, W_HEADS), F32)] * 8,
        compiler_params=_cparams(("parallel",)),
    )(u, up, sh, mu, w0, wup, a0, aup, gup, k_k, k_a, r_k, gs)


def _rwkv_chunks(chains, consts, cc, n_double):
    tri, m0, strict, incl, eye = consts

    def stack(x):
        return jnp.concatenate([jnp.where(m0, x, 0.0), jnp.where(m0, 0.0, x)], axis=0)

    cums = [_dot_exact_lhs(tri, ch[1]) for ch in chains]
    pre = []
    for (r, ld, k, v, kk, b, st), cum in zip(chains, cums):
        g_in = jnp.exp(cum)
        g_inv = jnp.exp(-cum)
        g_end = jnp.exp(cum[cc - 1:cc, :] - cum)
        ar = jnp.concatenate([stack(-kk * jnp.exp(cum - ld)), stack(r * g_in)], axis=0)
        pre.append(dict(ar=ar, b2=stack(b * g_inv), k2=stack(k * g_inv), v2=stack(v),
                        bg2t=jnp.transpose(stack(b * g_end)), kg2t=jnp.transpose(stack(k * g_end)),
                        g_col=jnp.transpose(jnp.broadcast_to(g_in[cc - 1:cc, :], (LANES, LANES))), st=st))
    mbs = [_dot3(c["ar"], c["b2"], _dot_nt) for c in pre]
    mks = [_dot1(c["ar"], c["k2"], _dot_nt) for c in pre]
    arss = [_dot1(c["ar"], c["st"]) for c in pre]
    m_abs = [jnp.where(strict, mb[0:2 * cc], 0.0) for mb in mbs]
    t_invs = [eye + m for m in m_abs]
    pws = m_abs
    for _ in range(n_double - 1):
        pws = [_dot3(pw, pw) for pw in pws]
        t_invs = [t + _dot3(pw, t) for pw, t in zip(pws, t_invs)]
    rhs = [ars[0:2 * cc] + _dot1(jnp.where(strict, mk[0:2 * cc], 0.0), c["v2"])
           for ars, mk, c in zip(arss, mks, pre)]
    z2s = [_dot3(t, x) for t, x in zip(t_invs, rhs)]
    outs = []
    for c, mb, mk, ars, z2 in zip(pre, mbs, mks, arss, z2s):
        n_rb = jnp.where(incl, mb[2 * cc:4 * cc], 0.0)
        n_rk = jnp.where(incl, mk[2 * cc:4 * cc], 0.0)
        y2 = ars[2 * cc:4 * cc] + _dot1(n_rb, z2) + _dot1(n_rk, c["v2"])
        st_new = c["st"] * c["g_col"] + _dot1(c["bg2t"], z2) + _dot1(c["kg2t"], c["v2"])
        outs.append((y2[0:cc] + y2[cc:2 * cc], st_new))
    return outs


def _rwkv_scan_kernel(r_ref, ld_ref, k_ref, v_ref, kk_ref, b_ref, s0_ref, y_ref, sT_ref, st_scr,
                      *, chunk, n_double, nb):
    c = pl.program_id(1)
    cc = chunk

    @pl.when(c == 0)
    def _():
        st_scr[...] = s0_ref[...]

    rr = _iota((2 * cc, 2 * cc), 0)
    cl = _iota((2 * cc, 2 * cc), 1)
    same = (rr // cc) == (cl // cc)
    consts = ((_iota((cc, cc), 0) >= _iota((cc, cc), 1)).astype(BF16),
              _iota((cc, LANES), 1) < HEAD_DIM,
              jnp.logical_and(same, (rr % cc) > (cl % cc)),
              jnp.logical_and(same, (rr % cc) >= (cl % cc)),
              jnp.where(rr == cl, 1.0, 0.0))
    chains = [(bi, p, slice(p * LANES, (p + 1) * LANES)) for bi in range(nb) for p in range(N_PAIRS)]
    loaded = [tuple(x[bi, :, sl] for x in (r_ref, ld_ref, k_ref, v_ref, kk_ref, b_ref)) + (st_scr[bi, p],)
              for bi, p, sl in chains]
    results = _rwkv_chunks(loaded, consts, cc, n_double)
    for (bi, p, sl), (y, st_new) in zip(chains, results):
        y_ref[bi, :, sl] = y
        st_scr[bi, p] = st_new

    @pl.when(c == pl.num_programs(1) - 1)
    def _():
        sT_ref[...] = st_scr[...]


def _rwkv_scan(r, ld, k, v, kk, b, s0_bd, nb):
    bsz, t, _ = r.shape
    cc = RWKV_CHUNK
    assert t % cc == 0 and bsz % nb == 0
    blk = lambda i, c: (i, c, 0)
    st = lambda i, c: (i, 0, 0, 0)
    return pl.pallas_call(
        functools.partial(_rwkv_scan_kernel, chunk=cc, n_double=int(math.log2(cc)), nb=nb),
        grid=(bsz // nb, t // cc),
        in_specs=[pl.BlockSpec((nb, cc, W_HEADS), blk)] * 6
                 + [pl.BlockSpec((nb, N_PAIRS, LANES, LANES), st)],
        out_specs=[pl.BlockSpec((nb, cc, W_HEADS), blk), pl.BlockSpec((nb, N_PAIRS, LANES, LANES), st)],
        out_shape=[jax.ShapeDtypeStruct((bsz, t, W_HEADS), F32),
                   jax.ShapeDtypeStruct((bsz, N_PAIRS, LANES, LANES), F32)],
        scratch_shapes=[pltpu.VMEM((nb, N_PAIRS, LANES, LANES), F32)],
        compiler_params=_cparams(("parallel", "arbitrary")),
    )(r, ld, k, v, kk, b, s0_bd)


def _merge_kernel(x_ref, yf_ref, ym_ref, yr_ref, bonus_ref, g_ref, gates_ref,
                  lw_ref, lb_ref, gm_ref, wf_ref, wr_ref, wm_ref, wo_ref, o_ref):
    d = x_ref.shape[1]
    gm = gm_ref[...]
    yr = yr_ref[...]
    mean = _head_reduce(yr, gm) * (1.0 / HEAD_DIM)
    dv = yr - mean
    var = _head_reduce(dv * dv, gm) * (1.0 / HEAD_DIM)
    yn = dv * lax.rsqrt(var + LNX_EPS) * lw_ref[...] + lb_ref[...]
    y_rwkv = (yn + bonus_ref[...]) * g_ref[...]
    gates = gates_ref[...]
    merged = (gates[:, 0:d] * _dot(yf_ref[...].astype(BF16), wf_ref[...])
              + gates[:, d:2 * d] * _dot(y_rwkv.astype(BF16), wr_ref[...])
              + gates[:, 2 * d:3 * d] * _dot(ym_ref[...].astype(BF16), wm_ref[...]))
    o_ref[...] = x_ref[...] + _dot(merged.astype(BF16), wo_ref[...])


def _merge(x, yf, ym, yr, bonus, g, gates, lw, lb, gm, wf, wr, wm, wo, tm=256):
    m, d = x.shape
    row = lambda i: (i, 0)
    fix = lambda i: (0, 0)
    act = pl.BlockSpec((tm, W_HEADS), row)
    vec = pl.BlockSpec((1, W_HEADS), fix)
    wbr = pl.BlockSpec((W_HEADS, d), fix)
    return pl.pallas_call(
        _merge_kernel,
        grid=(m // tm,),
        in_specs=[pl.BlockSpec((tm, d), row), act, act, act, act, act,
                  pl.BlockSpec((tm, N_BRANCH * d), row), vec, vec,
                  pl.BlockSpec((W_HEADS, W_HEADS), fix), wbr, wbr, wbr, pl.BlockSpec((d, d), fix)],
        out_specs=pl.BlockSpec((tm, d), row),
        out_shape=jax.ShapeDtypeStruct((m, d), F32),
        compiler_params=_cparams(("parallel",)),
    )(x, yf, ym, yr, bonus, g, gates, lw, lb, gm, wf, wr, wm, wo)


def _mlp_kernel(x_ref, g_ref, wu_ref, wd_ref, o_ref, h_scr, acc_scr):
    f = pl.program_id(1)

    @pl.when(f == 0)
    def _():
        h_scr[...] = _rms(x_ref[...], g_ref[...]).astype(BF16)
        acc_scr[...] = x_ref[...]

    a = jnp.maximum(_dot(h_scr[...], wu_ref[...]), 0.0)
    acc_scr[...] += _dot((a * a).astype(BF16), wd_ref[...])

    @pl.when(f == pl.num_programs(1) - 1)
    def _():
        o_ref[...] = acc_scr[...]


def _mlp(x, g, wu, wd, tf=2048):
    m, d = x.shape
    tm = 512 if m % 512 == 0 else 256
    dff = wu.shape[1]
    return pl.pallas_call(
        _mlp_kernel,
        grid=(m // tm, dff // tf),
        in_specs=[pl.BlockSpec((tm, d), lambda i, f: (i, 0)), pl.BlockSpec((1, d), lambda i, f: (0, 0)),
                  pl.BlockSpec((d, tf), lambda i, f: (0, f)), pl.BlockSpec((tf, d), lambda i, f: (f, 0))],
        out_specs=pl.BlockSpec((tm, d), lambda i, f: (i, 0)),
        out_shape=jax.ShapeDtypeStruct((m, d), F32),
        scratch_shapes=[pltpu.VMEM((tm, d), BF16), pltpu.VMEM((tm, d), F32)],
        compiler_params=_cparams(("parallel", "arbitrary")),
    )(x, g, wu, wd)


def _qbd_rows(q8):
    rows = jnp.concatenate([jnp.broadcast_to(q8[i:i + 1], (N_HEADS, W_HEADS))
                            for i in range(q8.shape[0])], axis=0)
    keep = (_iota(rows.shape, 1) // HEAD_DIM) == (_iota(rows.shape, 0) % N_HEADS)
    return jnp.where(keep, rows, 0.0)


def _rows_to_tokens(o, n_tok):
    keep = (_iota(o.shape, 1) // HEAD_DIM) == (_iota(o.shape, 0) % N_HEADS)
    return jnp.sum(jnp.where(keep, o, 0.0).reshape(n_tok, N_HEADS, W_HEADS), axis=1)


def _page_matrix(refs):
    mats = [r[0, 0].reshape(W_HEADS, r.shape[-1]).astype(BF16) for r in refs]
    return mats[0] if len(mats) == 1 else jnp.concatenate(mats, axis=1)


def _lane_prefix(x):
    lane = _iota(x.shape, 1)
    sh = 1
    while sh < x.shape[1]:
        x = x + jnp.where(lane >= sh, pltpu.roll(x, sh, 1), 0.0)
        sh *= 2
    return x


def _cum_pages(lf, carry):
    n_rows = lf.shape[0]
    loc = _lane_prefix(lf)
    tot = jnp.broadcast_to(loc[:, LANES - 1:LANES], loc.shape)
    run = tot
    rowi = _iota(loc.shape, 0)
    sh = N_HEADS
    while sh < n_rows:
        run = run + jnp.where(rowi >= sh, pltpu.roll(run, sh, 0), 0.0)
        sh *= 2
    carry_t = carry if n_rows == N_HEADS else jnp.concatenate([carry] * (n_rows // N_HEADS), axis=0)
    return loc + (run - tot) + carry_t, run[n_rows - N_HEADS:, :] + carry


def _tile_rows(x, n):
    return jnp.concatenate([x] * n, axis=0)


def _tile_lanes(x, n):
    return jnp.concatenate([x] * n, axis=1)


def _fox_dec_kernel(pt_ref, q_ref, kn_ref, vn_ref, lfn_ref, *refs, n_pg):
    k_refs = refs[0:n_pg]
    v_refs = refs[n_pg:2 * n_pg]
    lf_refs = refs[2 * n_pg:3 * n_pg]
    o_ref, m_scr, l_scr, acc_scr, carry_scr = refs[3 * n_pg:]
    j = pl.program_id(1)
    n_tok = q_ref.shape[1]

    @pl.when(j == 0)
    def _():
        m_scr[...] = jnp.full_like(m_scr, NEG)
        l_scr[...] = jnp.zeros_like(l_scr)
        acc_scr[...] = jnp.zeros_like(acc_scr)
        carry_scr[...] = jnp.zeros_like(carry_scr)

    qbd = _qbd_rows(q_ref[0].astype(F32)).astype(BF16)

    def attend(s, pv):
        m_new, alpha, l_new, pe = _softmax_update(s, m_scr[...], l_scr[...])
        l_scr[...] = l_new
        acc_scr[...] = _tile_lanes(alpha, W_HEADS // LANES) * acc_scr[...] + pv(pe)
        m_scr[...] = m_new

    lf = jnp.concatenate([r[0, 0] for r in lf_refs], axis=0)
    cum, carry = _cum_pages(lf, carry_scr[...])
    carry_scr[...] = carry
    kt = _page_matrix(k_refs)
    vt = _page_matrix(v_refs)
    bias = jnp.concatenate([_tile_rows(cum[i * N_HEADS:(i + 1) * N_HEADS], n_tok)
                            for i in range(n_pg)], axis=1)
    attend(_dot(qbd, kt) - bias * LOG2E, lambda pe: _dot_nt(pe, vt))

    @pl.when(j == pl.num_programs(1) - 1)
    def _():
        cn, _ = _cum_pages(lfn_ref[0], carry_scr[...])
        sn = _dot_nt(qbd, kn_ref[0].astype(BF16)) - _tile_rows(cn, n_tok) * LOG2E
        ok = _iota(sn.shape, 1) <= _iota(sn.shape, 0) // N_HEADS
        vn = vn_ref[0].astype(BF16)
        attend(jnp.where(ok, sn, NEG), lambda pe: _dot(pe, vn))
        l_tot = jnp.sum(l_scr[...], axis=-1, keepdims=True)
        o_ref[0] = _rows_to_tokens(acc_scr[...] / l_tot, n_tok)


def _page_specs(n_pg, tail_shape, layer):
    def one(i):
        return pl.BlockSpec((1, 1) + tail_shape,
                            lambda bb, j, pt, i=i: (pt[bb, j * n_pg + i], layer) + (0,) * len(tail_shape))
    return [one(i) for i in range(n_pg)]


def _fox_decode(page_table, q, kn, vn, lfn, cache_kt, cache_vt, cache_lft, layer):
    bsz, n_pages = page_table.shape
    page = cache_kt.shape[-1]
    n_tok = q.shape[1]
    rows = n_tok * N_HEADS
    n_pg = next(n for n in (16, 8, 4, 2, 1) if n_pages % n == 0)
    assert page == LANES and n_tok <= page
    seq = lambda bb, j, pt: (bb, 0, 0)
    in_specs = [pl.BlockSpec((1, n_tok, W_HEADS), seq), pl.BlockSpec((1, page, W_HEADS), seq),
                pl.BlockSpec((1, page, W_HEADS), seq), pl.BlockSpec((1, N_HEADS, page), seq)]
    in_specs += _page_specs(n_pg, (N_HEADS, HEAD_DIM, page), layer) * 2
    in_specs += _page_specs(n_pg, (N_HEADS, page), layer)
    grid_spec = pltpu.PrefetchScalarGridSpec(
        num_scalar_prefetch=1, grid=(bsz, n_pages // n_pg), in_specs=in_specs,
        out_specs=pl.BlockSpec((1, n_tok, W_HEADS), seq),
        scratch_shapes=[pltpu.VMEM((rows, LANES), F32), pltpu.VMEM((rows, LANES), F32),
                        pltpu.VMEM((rows, W_HEADS), F32), pltpu.VMEM((N_HEADS, LANES), F32)])
    return pl.pallas_call(
        functools.partial(_fox_dec_kernel, n_pg=n_pg),
        grid_spec=grid_spec,
        out_shape=jax.ShapeDtypeStruct((bsz, n_tok, W_HEADS), F32),
        compiler_params=_cparams(("parallel", "arbitrary")),
    )(page_table, q, kn, vn, lfn, *([cache_kt] * n_pg), *([cache_vt] * n_pg), *([cache_lft] * n_pg))


def _moba_dec_kernel(pt_ref, q_ref, kn_ref, vn_ref, *refs, n_pg, n_blk, bps):
    k_refs = refs[0:n_pg]
    v_refs = refs[n_pg:2 * n_pg]
    o_ref, g_scr, m_scr, l_scr, o_scr = refs[2 * n_pg:]
    n = pl.program_id(1)
    n_tok = q_ref.shape[1]
    rows = n_tok * N_HEADS
    lane = _iota((1, LANES), 1)

    @pl.when(n == 0)
    def _():
        g_scr[...] = jnp.zeros_like(g_scr)
        m_scr[...] = jnp.full_like(m_scr, NEG)
        l_scr[...] = jnp.zeros_like(l_scr)

    qh, ql = _split2(_qbd_rows(q_ref[0] * Q_SCALE))
    kt = _page_matrix(k_refs)
    vt = _page_matrix(v_refs)
    s2 = _dot(jnp.concatenate([qh, ql], axis=0), kt)
    bw = kt.shape[1] // bps
    g_new, m_new, l_new = g_scr[...], m_scr[...], l_scr[...]
    for bb in range(bps):
        blk = n * bps + bb
        s = s2[0:rows, bb * bw:(bb + 1) * bw]
        gate = jnp.sum(s + s2[rows:2 * rows, bb * bw:(bb + 1) * bw], axis=-1, keepdims=True)
        ch = _lane_chunks(s)
        mx = ch[0]
        for c in ch[1:]:
            mx = jnp.maximum(mx, c)
        m_n = jnp.max(mx, axis=-1, keepdims=True)
        ps = [jnp.exp2(c - m_n) for c in ch]
        lsum = ps[0]
        for p_ in ps[1:]:
            lsum = lsum + p_
        onehot = lane == blk
        g_new = jnp.where(onehot, gate, g_new)
        m_new = jnp.where(onehot, m_n, m_new)
        l_new = jnp.where(onehot, jnp.sum(lsum, axis=-1, keepdims=True), l_new)
        o_scr[blk] = _dot_nt(jnp.concatenate([p_.astype(BF16) for p_ in ps], axis=1),
                             vt[:, bb * bw:(bb + 1) * bw])
    g_scr[...] = g_new
    m_scr[...] = m_new
    l_scr[...] = l_new

    @pl.when(n == pl.num_programs(1) - 1)
    def _():
        sel = _top3_select(jnp.where(lane < n_blk, g_scr[...], NEG),
                           lane + jnp.zeros((rows, LANES), jnp.int32))
        sn = _dot_nt(qh, kn_ref[0].astype(BF16))
        ok = _iota(sn.shape, 1) <= _iota(sn.shape, 0) // N_HEADS
        sn = jnp.where(ok, sn, NEG)
        m_o = jnp.max(sn, axis=-1, keepdims=True)
        pn = jnp.exp2(sn - m_o)
        l_o = jnp.sum(pn, axis=-1, keepdims=True)
        o_o = _dot(pn.astype(BF16), vn_ref[0].astype(BF16))
        m_all = m_scr[...]
        m_tot = jnp.maximum(m_o, jnp.max(jnp.where(sel, m_all, NEG), axis=-1, keepdims=True))
        w = jnp.where(sel, jnp.exp2(m_all - m_tot), 0.0)
        e_o = jnp.exp2(m_o - m_tot)
        l_tot = l_o * e_o + jnp.sum(w * l_scr[...], axis=-1, keepdims=True)
        out = o_o * e_o
        for b_i in range(n_blk):
            out = out + w[:, b_i:b_i + 1] * o_scr[b_i]
        o_ref[0] = _rows_to_tokens(out / l_tot, n_tok)


def _moba_decode(page_table, q, kn, vn, cache_kt, cache_vt, layer):
    bsz, n_pages = page_table.shape
    page = cache_kt.shape[-1]
    assert MOBA_BLOCK % page == 0 and page == LANES
    pages_per_blk = MOBA_BLOCK // page
    assert n_pages % pages_per_blk == 0
    n_blk = n_pages // pages_per_blk
    assert n_blk <= LANES
    bps = next(n for n in (8, 4, 2, 1) if n_blk % n == 0)
    n_pg = bps * pages_per_blk
    n_tok = q.shape[1]
    rows = n_tok * N_HEADS
    seq = lambda bb, j, pt: (bb, 0, 0)
    in_specs = [pl.BlockSpec((1, n_tok, W_HEADS), seq), pl.BlockSpec((1, page, W_HEADS), seq),
                pl.BlockSpec((1, page, W_HEADS), seq)]
    in_specs += _page_specs(n_pg, (N_HEADS, HEAD_DIM, page), layer) * 2
    grid_spec = pltpu.PrefetchScalarGridSpec(
        num_scalar_prefetch=1, grid=(bsz, n_blk // bps), in_specs=in_specs,
        out_specs=pl.BlockSpec((1, n_tok, W_HEADS), seq),
        scratch_shapes=[pltpu.VMEM((rows, LANES), F32), pltpu.VMEM((rows, LANES), F32),
                        pltpu.VMEM((rows, LANES), F32), pltpu.VMEM((n_blk, rows, W_HEADS), F32)])
    return pl.pallas_call(
        functools.partial(_moba_dec_kernel, n_pg=n_pg, n_blk=n_blk, bps=bps),
        grid_spec=grid_spec,
        out_shape=jax.ShapeDtypeStruct((bsz, n_tok, W_HEADS), F32),
        compiler_params=_cparams(("parallel", "arbitrary")),
    )(page_table, q, kn, vn, *([cache_kt] * n_pg), *([cache_vt] * n_pg))


def _rope_tables(pos):
    half = ROPE_DIM // 2
    inv = ROPE_THETA ** (-jnp.arange(half, dtype=F32) / half)
    ang = pos.astype(F32)[:, None] * inv[None, :]
    cos, sin = jnp.cos(ang), jnp.sin(ang)
    t = pos.shape[0]
    one = jnp.ones((t, HEAD_DIM - ROPE_DIM), F32)
    zero = jnp.zeros((t, HEAD_DIM - ROPE_DIM), F32)
    z8 = jnp.zeros((t, half), F32)
    cos_h = jnp.concatenate([cos, cos, one], axis=1)
    sp_h = jnp.concatenate([z8, sin, zero], axis=1)
    sm_h = jnp.concatenate([-sin, z8, zero], axis=1)
    tile2 = lambda a: jnp.concatenate([a, a], axis=1)
    return tile2(cos_h), tile2(sp_h), tile2(sm_h)


def _layer_params(l, p):
    d = p["w_in"].shape[1]
    w_in = p["w_in"][l]
    o = 0
    w_fox = w_in[:, o:o + 3 * W_HEADS]; o += 3 * W_HEADS
    w_ff = w_in[:, o:o + N_HEADS]; o += N_HEADS
    w_moba = w_in[:, o:o + 3 * W_HEADS]; o += 3 * W_HEADS
    w_ur = w_in[:, o:o + RWKV_COLS]; o += RWKV_COLS
    w_ug = w_in[:, o:o + N_BRANCH * d]
    pad_l = lambda a, n: jnp.pad(a, ((0, 0), (0, n - a.shape[1])))
    tile_h = lambda a: jnp.tile(a, N_HEADS)[None, :]
    zeros64 = jnp.zeros((64, W_HEADS), F32)
    return dict(
        norm_mix=p["norm_mix"][l][None, :],
        w_fox=w_fox.astype(BF16), w_ff=pad_l(w_ff, LANES).astype(BF16),
        b_ff=pad_l(p["b_forget"][l][None, :], LANES),
        w_moba=w_moba.astype(BF16), w_ur=w_ur.astype(BF16), w_ug=w_ug.astype(BF16),
        qn_fox=tile_h(p["qn_fox"][l]), kn_fox=tile_h(p["kn_fox"][l]),
        qn_moba=tile_h(p["qn_moba"][l]), kn_moba=tile_h(p["kn_moba"][l]),
        mu=p["rwkv_mu"][l][None, :], w0=p["rwkv_w0"][l][None, :], a0=p["rwkv_a0"][l][None, :],
        w_up=jnp.concatenate([p["rwkv_w_up"][l], zeros64], axis=0).astype(BF16),
        a_up=jnp.concatenate([zeros64, p["rwkv_a_up"][l]], axis=0).astype(BF16),
        g_up=p["rwkv_g_up"][l].astype(BF16),
        k_k=p["rwkv_k_k"][l][None, :], k_a=p["rwkv_k_a"][l][None, :],
        r_k=p["rwkv_r_k"][l].reshape(1, W_HEADS),
        lnx_w=p["rwkv_lnx_w"][l][None, :], lnx_b=p["rwkv_lnx_b"][l][None, :],
        w_br_fox=p["w_br_fox"][l].astype(BF16), w_br_rwkv=p["w_br_rwkv"][l].astype(BF16),
        w_br_moba=p["w_br_moba"][l].astype(BF16), w_out=p["w_out"][l].astype(BF16),
        norm_mlp=p["norm_mlp"][l][None, :],
        w_up_mlp=p["w_mlp_up"][l].astype(BF16), w_down_mlp=p["w_mlp_down"][l].astype(BF16),
    )


def _state_to_blockdiag(s):
    b = s.shape[0]
    st = jnp.swapaxes(s, -1, -2).reshape(b, N_PAIRS, 2, HEAD_DIM, HEAD_DIM)
    z = jnp.zeros_like(st[:, :, 0])
    top = jnp.concatenate([st[:, :, 0], z], axis=-1)
    bot = jnp.concatenate([z, st[:, :, 1]], axis=-1)
    return jnp.concatenate([top, bot], axis=-2)


def _blockdiag_to_state(sbd):
    b = sbd.shape[0]
    s0 = sbd[:, :, 0:HEAD_DIM, 0:HEAD_DIM]
    s1 = sbd[:, :, HEAD_DIM:, HEAD_DIM:]
    st = jnp.stack([s0, s1], axis=2).reshape(b, N_HEADS, HEAD_DIM, HEAD_DIM)
    return jnp.swapaxes(st, -1, -2)


def _mixers_common(x2, lp, gs_ones, place, rope, shift0, s0, bsz, t, tiles_per_seq,
                   fox_stack=None, moba_stack=None):
    m = x2.shape[0]
    qs, qe, qo, ke, ko, fk, fv, fvb, logf = _proj_fox(
        x2, lp["norm_mix"], lp["w_fox"], lp["w_ff"], lp["b_ff"], lp["qn_fox"], lp["kn_fox"],
        gs_ones, place, tiles_per_seq, stack=fox_stack)
    mq, mk, mv, mvb = _proj_moba(x2, lp["norm_mix"], lp["w_moba"], lp["qn_moba"], lp["kn_moba"],
                                 gs_ones, *rope, tiles_per_seq, stack=moba_stack)
    ur = _proj_plain(x2, lp["norm_mix"], lp["w_ur"])
    gates = _proj_plain(x2, lp["norm_mix"], lp["w_ug"], act="sigmoid")
    ur3 = ur.reshape(bsz, t, RWKV_COLS)
    r, ld, k2, v, kk, b, bonus, g = _rwkv_prep(
        ur, shift0, t, lp["mu"], lp["w0"], lp["w_up"], lp["a0"], lp["a_up"], lp["g_up"],
        lp["k_k"], lp["k_a"], lp["r_k"], gs_ones)
    t_pad = -(-t // RWKV_CHUNK) * RWKV_CHUNK
    seqs = [a.reshape(bsz, t, W_HEADS) for a in (r, ld, k2, v, kk, b)]
    if t_pad != t:
        seqs = [jnp.pad(a, ((0, 0), (0, t_pad - t), (0, 0))) for a in seqs]
    y_r, s_bd = _rwkv_scan(*seqs, _state_to_blockdiag(s0), nb=2 if bsz % 2 == 0 else 1)
    y_r = y_r[:, :t].reshape(m, W_HEADS)
    return dict(qs=qs, qe=qe, qo=qo, ke=ke, ko=ko, fk=fk, fv=fv, fvb=fvb, logf=logf,
                mq=mq, mk=mk, mv=mv, mvb=mvb, gates=gates,
                y_r=y_r, bonus=bonus, g=g, s_new=_blockdiag_to_state(s_bd), shift_new=ur3[:, -1])


def _finish_layer(x2, lp, gs_ones, pc, y_fox, y_moba):
    xo = _merge(x2, y_fox, y_moba, pc["y_r"], pc["bonus"], pc["g"], pc["gates"],
                lp["lnx_w"], lp["lnx_b"], gs_ones, lp["w_br_fox"], lp["w_br_rwkv"],
                lp["w_br_moba"], lp["w_out"])
    return _mlp(xo, lp["norm_mlp"], lp["w_up_mlp"], lp["w_down_mlp"])


def _flash_tile(t):
    for tile in (1024, 512, MOBA_BLOCK):
        if t % tile == 0:
            return tile
    raise ValueError("prompt length must be a multiple of the MoBA block")


def kernel(x_prompt, x_sample, cache_fox_k, cache_fox_v, cache_fox_logf, cache_moba_k, cache_moba_v, state_rwkv, state_rwkv_shift, page_table, norm_mix, w_in, b_forget, qn_fox, kn_fox, qn_moba, kn_moba, rwkv_mu, rwkv_w0, rwkv_w_up, rwkv_a0, rwkv_a_up, rwkv_g_up, rwkv_k_k, rwkv_k_a, rwkv_r_k, rwkv_lnx_w, rwkv_lnx_b, w_br_fox, w_br_rwkv, w_br_moba, w_out, norm_mlp, w_mlp_up, w_mlp_down):
    params = dict(norm_mix=norm_mix, w_in=w_in, b_forget=b_forget, qn_fox=qn_fox, kn_fox=kn_fox,
                  qn_moba=qn_moba, kn_moba=kn_moba, rwkv_mu=rwkv_mu, rwkv_w0=rwkv_w0,
                  rwkv_w_up=rwkv_w_up, rwkv_a0=rwkv_a0, rwkv_a_up=rwkv_a_up, rwkv_g_up=rwkv_g_up,
                  rwkv_k_k=rwkv_k_k, rwkv_k_a=rwkv_k_a, rwkv_r_k=rwkv_r_k, rwkv_lnx_w=rwkv_lnx_w,
                  rwkv_lnx_b=rwkv_lnx_b, w_br_fox=w_br_fox, w_br_rwkv=w_br_rwkv,
                  w_br_moba=w_br_moba, w_out=w_out, norm_mlp=norm_mlp, w_mlp_up=w_mlp_up,
                  w_mlp_down=w_mlp_down)
    depth = w_in.shape[0]
    bp, tp, d = x_prompt.shape
    bs, ts, _ = x_sample.shape
    n_pages = page_table.shape[1]
    page = cache_fox_k.shape[2]
    past_len = n_pages * page
    assert past_len % MOBA_BLOCK == 0 and ts <= MOBA_BLOCK and tp % MOBA_BLOCK == 0
    tile = _flash_tile(tp)
    tm = 256
    assert (bp * tp) % tm == 0 and tp % tm == 0 and (bs * ts) % tm == 0

    hid = jnp.arange(W_HEADS) // HEAD_DIM
    gs_ones = (hid[:, None] == hid[None, :]).astype(BF16)
    place = _bias_placement()
    rope_p = _rope_tables(jnp.arange(tp, dtype=jnp.int32))
    rope_s = tuple(jnp.tile(a, (bs, 1)) for a in _rope_tables(past_len + jnp.arange(ts, dtype=jnp.int32)))
    tview = lambda c: jnp.transpose(c, (0, 1, 3, 4, 2))
    fox_kt, fox_vt, moba_kt, moba_vt = (tview(c) for c in (cache_fox_k, cache_fox_v, cache_moba_k, cache_moba_v))
    fox_lft = jnp.transpose(cache_fox_logf, (0, 1, 3, 2))

    xp = x_prompt.reshape(bp * tp, d)
    xs = x_sample.reshape(bs * ts, d)
    rows_p = [[] for _ in range(3)]
    rows_s = [[] for _ in range(7)]
    n_stack = bp * depth * tp
    fk_all, fv_all, mk_all, mv_all = (jnp.zeros((n_stack, W_HEADS), F32) for _ in range(4))
    for l in range(depth):
        lp = _layer_params(l, params)

        pc = _mixers_common(xp, lp, gs_ones, place, rope_p,
                            jnp.zeros((bp, RWKV_COLS), F32),
                            jnp.zeros((bp, N_HEADS, HEAD_DIM, HEAD_DIM), F32), bp, tp, tp // tm,
                            fox_stack=(fk_all, fv_all, l, depth, n_stack),
                            moba_stack=(mk_all, mv_all, l, depth, n_stack))
        fk_all, fv_all, mk_all, mv_all = pc["fk"], pc["fv"], pc["mk"], pc["mv"]
        b3 = lambda a: a.reshape(bp, tp, W_HEADS)
        y_fox = _flash(b3(pc["qe"]), b3(pc["qo"]), b3(pc["ke"]), b3(pc["ko"]), b3(pc["fvb"]), tile)
        mqe, mqo, mke, mko = _moba_gate(b3(pc["mq"]), mk_all.reshape(bp, depth * tp, W_HEADS),
                                        l * (tp // MOBA_BLOCK))
        y_moba = _flash(mqe, mqo, mke, mko, b3(pc["mvb"]), tile)
        xp = _finish_layer(xp, lp, gs_ones, pc, y_fox.reshape(bp * tp, W_HEADS),
                           y_moba.reshape(bp * tp, W_HEADS))
        for acc, val in zip(rows_p, (pc["logf"].reshape(bp, tp, N_HEADS), pc["s_new"], pc["shift_new"])):
            acc.append(val)

        sc = _mixers_common(xs, lp, gs_ones, place, rope_s, state_rwkv_shift[:, l],
                            state_rwkv[:, l], bs, ts, 1)
        s3 = lambda a: a.reshape(bs, ts, W_HEADS)
        padt = lambda a: jnp.pad(s3(a), ((0, 0), (0, page - ts), (0, 0)))
        lfn = jnp.pad(jnp.swapaxes(sc["logf"].reshape(bs, ts, N_HEADS), 1, 2),
                      ((0, 0), (0, 0), (0, page - ts)))
        y_fox_s = _fox_decode(page_table, s3(sc["qs"]), padt(sc["fk"]), padt(sc["fv"]), lfn,
                              fox_kt, fox_vt, fox_lft, l)
        y_moba_s = _moba_decode(page_table, s3(sc["mq"]), padt(sc["mk"]), padt(sc["mv"]),
                                moba_kt, moba_vt, l)
        xs = _finish_layer(xs, lp, gs_ones, sc, y_fox_s.reshape(bs * ts, W_HEADS),
                           y_moba_s.reshape(bs * ts, W_HEADS))
        h5s = lambda a: a.reshape(bs, ts, N_HEADS, HEAD_DIM)
        for acc, val in zip(rows_s, (h5s(sc["fk"]), h5s(sc["fv"]), sc["logf"].reshape(bs, ts, N_HEADS),
                                     h5s(sc["mk"]), h5s(sc["mv"]), sc["s_new"], sc["shift_new"])):
            acc.append(val)

    p_logf, p_rwkv, p_shift = (jnp.stack(a, axis=1) for a in rows_p)
    kv5 = lambda a: a.reshape(bp, depth, tp, N_HEADS, HEAD_DIM)
    outs_s = [jnp.stack(a, axis=1) for a in rows_s]
    return (xp.reshape(bp, tp, d), xs.reshape(bs, ts, d), kv5(fk_all), kv5(fv_all), p_logf,
            kv5(mk_all), kv5(mv_all), p_rwkv, p_shift, *outs_s)
```
